```python
import math
import jax, jax.numpy as jnp
from jax import lax
import numpy as np

D_MODEL = 1024
BATCH = 8
SEQ = 2048
DEPTH = 1
DEC_BATCH = 128
DEC_SEQ = 4
PAST_LEN = 16384
PAGE_SIZE = 128

D_LRU = D_MODEL
LRU_HEADS = 16
LRU_HEAD_DIM = D_LRU // LRU_HEADS
LRU_CONV = 4
LRU_C = 8.0
S5_GROUP = 16
D_S5 = D_MODEL // 2
S5_GROUPS = D_S5 // S5_GROUP
S5_STATE = 64
D_FF = 3 * D_MODEL
FFN_CONV = 3
D_IN = D_LRU + D_S5 + 2 * D_MODEL
ALPHA = (2.0 * DEPTH) ** 0.25
BETA = (8.0 * DEPTH) ** -0.25
LN_EPS = 1e-5

kernel_name = 'hybrid_rglru_s5_convffn_deepnorm_step'


def layer_norm(x, g, b):
    xf = x.astype(jnp.float32)
    mu = jnp.mean(xf, axis=-1, keepdims=True)
    var = jnp.mean(jnp.square(xf - mu), axis=-1, keepdims=True)
    return ((xf - mu) * lax.rsqrt(var + LN_EPS) * g + b).astype(x.dtype)


def causal_dwconv(x, buf, w, b):
    width = w.shape[0]
    L = x.shape[1]
    xx = jnp.concatenate([buf.astype(x.dtype), x], axis=1)
    out = b + xx[:, 0:L] * w[0]
    for k in range(1, width):
        out = out + xx[:, k:k + L] * w[k]
    return out.astype(x.dtype), xx[:, xx.shape[1] - (width - 1):]


def _lin_combine(c1, c2):
    a1, b1 = c1
    a2, b2 = c2
    return a1 * a2, a2 * b1 + b2


def _cplx_combine(c1, c2):
    ar1, ai1, br1, bi1 = c1
    ar2, ai2, br2, bi2 = c2
    ar = ar1 * ar2 - ai1 * ai2
    ai = ar1 * ai2 + ai1 * ar2
    br = ar2 * br1 - ai2 * bi1 + br2
    bi = ar2 * bi1 + ai2 * br1 + bi2
    return ar, ai, br, bi


def rg_lru(x, h0, w_a, b_a, w_x, b_x, lam):
    B_, L, _ = x.shape
    xf = x.astype(jnp.float32)
    xh = xf.reshape(B_, L, LRU_HEADS, LRU_HEAD_DIM)
    r = jax.nn.sigmoid(jnp.einsum('blhi,hij->blhj', xh, w_a.astype(jnp.float32)).reshape(B_, L, D_LRU) + b_a)
    i = jax.nn.sigmoid(jnp.einsum('blhi,hij->blhj', xh, w_x.astype(jnp.float32)).reshape(B_, L, D_LRU) + b_x)
    log_a = LRU_C * r * jax.nn.log_sigmoid(lam.astype(jnp.float32))
    a = jnp.exp(log_a)
    mult = jnp.sqrt(jnp.maximum(-jnp.expm1(2.0 * log_a), 0.0))
    bterm = mult * (i * xf)
    a_cum, h = lax.associative_scan(_lin_combine, (a, bterm), axis=1)
    h = h + a_cum * h0.astype(jnp.float32)[:, None]
    return h, h[:, -1]


def s5_ssm(u, h0_re, h0_im, a_re, a_im, log_dt, b_re, b_im, c_re, c_im, d):
    B_, L, _ = u.shape
    f32 = jnp.float32
    uf = u.astype(f32).reshape(B_, L, S5_GROUPS, S5_GROUP)
    a_re = a_re.astype(f32)
    a_im = a_im.astype(f32)
    dt = jnp.exp(log_dt.astype(f32))[:, None]
    mag = jnp.exp(a_re * dt)
    ab_re = mag * jnp.cos(a_im * dt)
    ab_im = mag * jnp.sin(a_im * dt)
    nr = ab_re - 1.0
    ni = ab_im
    den = a_re * a_re + a_im * a_im
    coef_re = (nr * a_re + ni * a_im) / den
    coef_im = (ni * a_re - nr * a_im) / den
    b_re = b_re.astype(f32)
    b_im = b_im.astype(f32)
    bb_re = coef_re[..., None] * b_re - coef_im[..., None] * b_im
    bb_im = coef_re[..., None] * b_im + coef_im[..., None] * b_re
    bu_re = jnp.einsum('blgc,gpc->blgp', uf, bb_re)
    bu_im = jnp.einsum('blgc,gpc->blgp', uf, bb_im)
    ar = jnp.broadcast_to(ab_re, bu_re.shape)
    ai = jnp.broadcast_to(ab_im, bu_re.shape)
    acr, aci, hr, hi = lax.associative_scan(_cplx_combine, (ar, ai, bu_re, bu_im), axis=1)
    h0r = h0_re.astype(f32)[:, None]
    h0i = h0_im.astype(f32)[:, None]
    hr = hr + acr * h0r - aci * h0i
    hi = hi + acr * h0i + aci * h0r
    y = jnp.einsum('blgp,gcp->blgc', hr, c_re.astype(f32)) - jnp.einsum('blgp,gcp->blgc', hi, c_im.astype(f32))
    y = y.reshape(B_, L, D_S5) + d * u.astype(f32)
    return y, hr[:, -1], hi[:, -1]


def trunk_layer(x, lru_conv_buf, lru_h0, s5_h0_re, s5_h0_im, ffn_conv_buf,
                w_in, lru_conv_w, lru_conv_b, lru_wa, lru_ba, lru_wx, lru_bx, lru_lambda,
                s5_a_re, s5_a_im, s5_log_dt, s5_b_re, s5_b_im, s5_c_re, s5_c_im, s5_d,
                w_glu, w_out, ln1_g, ln1_b, w_up, ffn_conv_w, ffn_conv_b, w_down, ln2_g, ln2_b):
    proj = x @ w_in
    x_lru = proj[..., :D_LRU]
    u_s5 = proj[..., D_LRU:D_LRU + D_S5]
    gates = jax.nn.sigmoid(proj[..., D_LRU + D_S5:].astype(jnp.float32))
    g_lru = gates[..., :D_MODEL]
    g_s5 = gates[..., D_MODEL:]
    xc, new_lru_conv = causal_dwconv(x_lru, lru_conv_buf, lru_conv_w, lru_conv_b)
    h_lru, new_lru_h = rg_lru(xc, lru_h0, lru_wa, lru_ba, lru_wx, lru_bx, lru_lambda)
    y_s5, new_s5_re, new_s5_im = s5_ssm(u_s5, s5_h0_re, s5_h0_im, s5_a_re, s5_a_im, s5_log_dt,
                                        s5_b_re, s5_b_im, s5_c_re, s5_c_im, s5_d)
    z = jax.nn.gelu(y_s5).astype(x.dtype)
    glu = z @ w_glu
    s5_out = glu[..., :D_MODEL].astype(jnp.float32) * jax.nn.sigmoid(glu[..., D_MODEL:].astype(jnp.float32))
    merged = (g_lru * h_lru + g_s5 * s5_out).astype(x.dtype)
    mix = merged @ w_out
    x = layer_norm(ALPHA * x + mix, ln1_g, ln1_b)
    up = x @ w_up
    a = up[..., :D_FF]
    gate = up[..., D_FF:]
    ac, new_ffn_conv = causal_dwconv(a, ffn_conv_buf, ffn_conv_w, ffn_conv_b)
    f = (jax.nn.gelu(ac) * gate) @ w_down
    x = layer_norm(ALPHA * x + f, ln2_g, ln2_b)
    return x, new_lru_conv, new_lru_h, new_s5_re, new_s5_im, new_ffn_conv


def setup_inputs(seed: int = 0) -> dict:
    key = jax.random.key(seed)
    ks = jax.random.split(key, 40)
    f32 = jnp.float32

    def nrm(k, shape, s):
        return jax.random.normal(k, shape, f32) * s

    a_base = jax.random.uniform(ks[7], (DEPTH, D_LRU), f32, minval=0.9, maxval=0.999)
    sig = a_base ** (1.0 / LRU_C)
    lru_lambda = jnp.log(sig) - jnp.log1p(-sig)
    n_idx = jnp.arange(S5_STATE, dtype=f32)
    s5_log_dt = jax.random.uniform(ks[10], (DEPTH, S5_GROUPS), f32,
                                   minval=math.log(1e-3), maxval=math.log(1e-1))
    return {
        'x_prompt': nrm(ks[0], (BATCH, SEQ, D_MODEL), 1.0),
        'x_sample': nrm(ks[1], (DEC_BATCH, DEC_SEQ, D_MODEL), 1.0),
        'state_lru_conv': nrm(ks[2], (DEPTH, DEC_BATCH, LRU_CONV - 1, D_LRU), 1.0),
        'state_lru_h': nrm(ks[3], (DEPTH, DEC_BATCH, D_LRU), 0.5),
        'state_s5_re': nrm(ks[4], (DEPTH, DEC_BATCH, S5_GROUPS, S5_STATE), 0.1),
        'state_s5_im': nrm(ks[5], (DEPTH, DEC_BATCH, S5_GROUPS, S5_STATE), 0.1),
        'state_ffn_conv': nrm(ks[6], (DEPTH, DEC_BATCH, FFN_CONV - 1, D_FF), 1.0),
        'w_in': nrm(ks[11], (DEPTH, D_MODEL, D_IN), D_MODEL ** -0.5),
        'lru_conv_w': nrm(ks[12], (DEPTH, LRU_CONV, D_LRU), LRU_CONV ** -0.5),
        'lru_conv_b': nrm(ks[13], (DEPTH, D_LRU), 0.02),
        'lru_wa': nrm(ks[14], (DEPTH, LRU_HEADS, LRU_HEAD_DIM, LRU_HEAD_DIM), LRU_HEAD_DIM ** -0.5),
        'lru_ba': nrm(ks[15], (DEPTH, D_LRU), 0.02),
        'lru_wx': nrm(ks[16], (DEPTH, LRU_HEADS, LRU_HEAD_DIM, LRU_HEAD_DIM), LRU_HEAD_DIM ** -0.5),
        'lru_bx': nrm(ks[17], (DEPTH, D_LRU), 0.02),
        'lru_lambda': lru_lambda,
        's5_a_re': -0.5 + nrm(ks[8], (DEPTH, S5_GROUPS, S5_STATE), 0.01),
        's5_a_im': math.pi * n_idx + nrm(ks[9], (DEPTH, S5_GROUPS, S5_STATE), 0.01),
        's5_log_dt': s5_log_dt,
        's5_b_re': nrm(ks[18], (DEPTH, S5_GROUPS, S5_STATE, S5_GROUP), (2.0 * S5_GROUP) ** -0.5),
        's5_b_im': nrm(ks[19], (DEPTH, S5_GROUPS, S5_STATE, S5_GROUP), (2.0 * S5_GROUP) ** -0.5),
        's5_c_re': nrm(ks[20], (DEPTH, S5_GROUPS, S5_GROUP, S5_STATE), S5_STATE ** -0.5),
        's5_c_im': nrm(ks[21], (DEPTH, S5_GROUPS, S5_GROUP, S5_STATE), S5_STATE ** -0.5),
        's5_d': nrm(ks[22], (DEPTH, D_S5), 1.0),
        'w_glu': nrm(ks[23], (DEPTH, D_S5, 2 * D_MODEL), D_S5 ** -0.5),
        'w_out': nrm(ks[24], (DEPTH, D_MODEL, D_MODEL), BETA * D_MODEL ** -0.5),
        'ln1_g': 1.0 + nrm(ks[25], (DEPTH, D_MODEL), 0.02),
        'ln1_b': nrm(ks[26], (DEPTH, D_MODEL), 0.02),
        'w_up': nrm(ks[27], (DEPTH, D_MODEL, 2 * D_FF), D_MODEL ** -0.5),
        'ffn_conv_w': nrm(ks[28], (DEPTH, FFN_CONV, D_FF), FFN_CONV ** -0.5),
        'ffn_conv_b': nrm(ks[29], (DEPTH, D_FF), 0.02),
        'w_down': nrm(ks[30], (DEPTH, D_FF, D_MODEL), BETA * D_FF ** -0.5),
        'ln2_g': 1.0 + nrm(ks[31], (DEPTH, D_MODEL), 0.02),
        'ln2_b': nrm(ks[32], (DEPTH, D_MODEL), 0.02),
    }


def reference(x_prompt, x_sample, state_lru_conv, state_lru_h, state_s5_re, state_s5_im, state_ffn_conv,
              w_in, lru_conv_w, lru_conv_b, lru_wa, lru_ba, lru_wx, lru_bx, lru_lambda,
              s5_a_re, s5_a_im, s5_log_dt, s5_b_re, s5_b_im, s5_c_re, s5_c_im, s5_d,
              w_glu, w_out, ln1_g, ln1_b, w_up, ffn_conv_w, ffn_conv_b, w_down, ln2_g, ln2_b):
    n_p = x_prompt.shape[0]
    xp = x_prompt
    xs = x_sample
    p_conv, p_h, p_re, p_im, p_ffn = [], [], [], [], []
    s_conv, s_h, s_re, s_im, s_ffn = [], [], [], [], []
    for l in range(DEPTH):
        w = (w_in[l], lru_conv_w[l], lru_conv_b[l], lru_wa[l], lru_ba[l], lru_wx[l], lru_bx[l], lru_lambda[l],
             s5_a_re[l], s5_a_im[l], s5_log_dt[l], s5_b_re[l], s5_b_im[l], s5_c_re[l], s5_c_im[l], s5_d[l],
             w_glu[l], w_out[l], ln1_g[l], ln1_b[l], w_up[l], ffn_conv_w[l], ffn_conv_b[l], w_down[l],
             ln2_g[l], ln2_b[l])
        zc = jnp.zeros((n_p, LRU_CONV - 1, D_LRU), xp.dtype)
        zh = jnp.zeros((n_p, D_LRU), jnp.float32)
        zs = jnp.zeros((n_p, S5_GROUPS, S5_STATE), jnp.float32)
        zf = jnp.zeros((n_p, FFN_CONV - 1, D_FF), xp.dtype)
        xp, c1, h1, r1, i1, f1 = trunk_layer(xp, zc, zh, zs, zs, zf, *w)
        p_conv.append(c1); p_h.append(h1); p_re.append(r1); p_im.append(i1); p_ffn.append(f1)
        xs, c2, h2, r2, i2, f2 = trunk_layer(xs, state_lru_conv[l], state_lru_h[l], state_s5_re[l],
                                             state_s5_im[l], state_ffn_conv[l], *w)
        s_conv.append(c2); s_h.append(h2); s_re.append(r2); s_im.append(i2); s_ffn.append(f2)
    return (xp, xs,
            jnp.stack(p_conv), jnp.stack(p_h), jnp.stack(p_re), jnp.stack(p_im), jnp.stack(p_ffn),
            jnp.stack(s_conv), jnp.stack(s_h), jnp.stack(s_re), jnp.stack(s_im), jnp.stack(s_ffn))
```

```python
import functools
import math

import jax
import jax.numpy as jnp
from jax.experimental import pallas as pl
from jax.experimental.pallas import tpu as pltpu

D_MODEL = 1024
LRU_HEADS = 16
LRU_HEAD_DIM = D_MODEL // LRU_HEADS
LRU_CONV = 4
LRU_C = 8.0
S5_GROUP = 16
D_S5 = D_MODEL // 2
S5_GROUPS = D_S5 // S5_GROUP
S5_STATE = 64
D_STATE = S5_GROUPS * S5_STATE
D_FF = 3 * D_MODEL
FFN_CONV = 3
DEPTH = 1
ALPHA = (2.0 * DEPTH) ** 0.25
LN_EPS = 1e-5

MXU_TILE = 256
GATE_BLOCKS = D_MODEL // MXU_TILE
S5_IN_BLOCK = 128
S5_BLOCKS = D_S5 // S5_IN_BLOCK
S5_STATE_BLOCK = D_STATE // S5_BLOCKS
VMEM_LIMIT_BYTES = 56 * 1024 * 1024

_BF16 = jnp.bfloat16
_F32 = jnp.float32


def _dot(a, b):
    return jnp.dot(a, b, preferred_element_type=_F32)


def _layer_norm(y, g, b):
    mu = jnp.mean(y, axis=-1, keepdims=True)
    yc = y - mu
    var = jnp.mean(yc * yc, axis=-1, keepdims=True)
    return yc * jax.lax.rsqrt(var + LN_EPS) * g + b


def _mixer_kernel(nb, tt,
                  x_ref, w_in_ref, cw_ref, cb_ref, wg_ref, ba_ref, bx_ref, lam_ref,
                  wb_ref, are_ref, aim_ref, wcr_ref, wci_ref, d_ref, wglu_ref, wout_ref,
                  g1_ref, b1_ref, conv0_ref, h0_ref, sre0_ref, sim0_ref,
                  x1_ref, conv_out_ref, h_ref, sre_ref, sim_ref,
                  xbuf, abuf, hbuf, bure, buim):
    m = nb * tt
    hist = (LRU_CONV - 1) * nb

    @pl.when(pl.program_id(0) == 0)
    def _():
        xbuf[0:hist, :] = conv0_ref[...]
        h_ref[...] = h0_ref[...]
        sre_ref[...] = sre0_ref[...]
        sim_ref[...] = sim0_ref[...]

    x = x_ref[...]
    xb = x.astype(_BF16)

    xbuf[hist:hist + m, :] = _dot(xb, w_in_ref[:, 0:D_MODEL])
    xc = cb_ref[...] + xbuf[0:m, :] * cw_ref[0:1, :]
    for k in range(1, LRU_CONV):
        xc = xc + xbuf[k * nb:k * nb + m, :] * cw_ref[k:k + 1, :]
    new_hist = xbuf[m:m + hist, :]
    xbuf[0:hist, :] = new_hist
    conv_out_ref[...] = new_hist

    xcb = xc.astype(_BF16)
    lam = lam_ref[...]
    c_lam = LRU_C * (jnp.minimum(lam, 0.0) - jnp.log1p(jnp.exp(-jnp.abs(lam))))
    for k in range(GATE_BLOCKS):
        blk = slice(k * MXU_TILE, (k + 1) * MXU_TILE)
        g = _dot(xcb[:, blk], wg_ref[k])
        r = jax.nn.sigmoid(g[:, 0:MXU_TILE] + ba_ref[:, blk])
        ig = jax.nn.sigmoid(g[:, MXU_TILE:] + bx_ref[:, blk])
        log_a = r * c_lam[:, blk]
        abuf[:, blk] = jnp.exp(log_a)
        th = jnp.tanh(log_a)
        mult = jnp.sqrt(jnp.maximum(-2.0 * th / (1.0 - th), 0.0))
        hbuf[:, blk] = mult * (ig * xc[:, blk])

    h = h_ref[...]
    for t in range(tt):
        rows = slice(t * nb, (t + 1) * nb)
        h = abuf[rows, :] * h + hbuf[rows, :]
        hbuf[rows, :] = h
    h_ref[...] = h

    u = _dot(xb, w_in_ref[:, D_MODEL:D_MODEL + D_S5])
    ub = u.astype(_BF16)
    for k in range(S5_BLOCKS):
        bu = _dot(ub[:, k * S5_IN_BLOCK:(k + 1) * S5_IN_BLOCK], wb_ref[k])
        cols = slice(k * S5_STATE_BLOCK, (k + 1) * S5_STATE_BLOCK)
        bure[:, cols] = bu[:, 0:S5_STATE_BLOCK]
        buim[:, cols] = bu[:, S5_STATE_BLOCK:]

    for k in range(S5_BLOCKS):
        cols = slice(k * S5_STATE_BLOCK, (k + 1) * S5_STATE_BLOCK)
        ar = jnp.broadcast_to(are_ref[:, cols], (nb, S5_STATE_BLOCK))
        ai = jnp.broadcast_to(aim_ref[:, cols], (nb, S5_STATE_BLOCK))
        hr = sre_ref[:, cols]
        hi = sim_ref[:, cols]
        for t in range(tt):
            rows = slice(t * nb, (t + 1) * nb)
            hr_new = ar * hr - ai * hi + bure[rows, cols]
            hi_new = ar * hi + ai * hr + buim[rows, cols]
            hr, hi = hr_new, hi_new
            bure[rows, cols] = hr
            buim[rows, cols] = hi
        sre_ref[:, cols] = hr
        sim_ref[:, cols] = hi

    ys = []
    for k in range(S5_BLOCKS):
        cols = slice(k * S5_STATE_BLOCK, (k + 1) * S5_STATE_BLOCK)
        ys.append(_dot(bure[:, cols].astype(_BF16), wcr_ref[k])
                  - _dot(buim[:, cols].astype(_BF16), wci_ref[k]))
    y = jnp.concatenate(ys, axis=1) + d_ref[...] * u
    z = jax.nn.gelu(y).astype(_BF16)
    glu = _dot(z, wglu_ref[...])
    s5_out = glu[:, 0:D_MODEL] * jax.nn.sigmoid(glu[:, D_MODEL:])

    g_lru = jax.nn.sigmoid(_dot(xb, w_in_ref[:, D_MODEL + D_S5:2 * D_MODEL + D_S5]))
    g_s5 = jax.nn.sigmoid(_dot(xb, w_in_ref[:, 2 * D_MODEL + D_S5:]))
    merged = (g_lru * hbuf[...] + g_s5 * s5_out).astype(_BF16)
    mix = _dot(merged, wout_ref[...])
    x1_ref[...] = _layer_norm(ALPHA * x + mix, g1_ref[...], b1_ref[...])


def _ffn_kernel(nb, tt,
                x_ref, wup_ref, cw_ref, cb_ref, wdown_ref, g2_ref, b2_ref, conv0_ref,
                out_ref, conv_out_ref,
                abuf):
    m = nb * tt
    hist = (FFN_CONV - 1) * nb

    @pl.when(pl.program_id(0) == 0)
    def _():
        abuf[0:hist, :] = conv0_ref[...]

    x = x_ref[...]
    xb = x.astype(_BF16)
    abuf[hist:hist + m, :] = _dot(xb, wup_ref[:, 0:D_FF])
    f = None
    for c in range(D_FF // D_MODEL):
        cols = slice(c * D_MODEL, (c + 1) * D_MODEL)
        ac = cb_ref[:, cols] + abuf[0:m, cols] * cw_ref[0:1, cols]
        for k in range(1, FFN_CONV):
            ac = ac + abuf[k * nb:k * nb + m, cols] * cw_ref[k:k + 1, cols]
        gate = _dot(xb, wup_ref[:, D_FF + c * D_MODEL:D_FF + (c + 1) * D_MODEL])
        hmid = (jax.nn.gelu(ac) * gate).astype(_BF16)
        part = _dot(hmid, wdown_ref[cols, :])
        f = part if f is None else f + part
    new_hist = abuf[m:m + hist, :]
    abuf[0:hist, :] = new_hist
    conv_out_ref[...] = new_hist
    out_ref[...] = _layer_norm(ALPHA * x + f, g2_ref[...], b2_ref[...])


def _resident(shape):
    zeros = (0,) * len(shape)
    return pl.BlockSpec(shape, lambda i: zeros, pipeline_mode=pl.Buffered(1))


def _run_mixer(x_rows, nb, tt, weights, states):
    n_rows = x_rows.shape[0]
    m = nb * tt
    hist = (LRU_CONV - 1) * nb
    tile = pl.BlockSpec((m, D_MODEL), lambda i: (i, 0))
    operands = (x_rows,) + tuple(weights) + tuple(states)
    in_specs = [tile] + [_resident(a.shape) for a in operands[1:]]
    out_shape = (
        jax.ShapeDtypeStruct((n_rows, D_MODEL), _F32),
        jax.ShapeDtypeStruct((hist, D_MODEL), _F32),
        jax.ShapeDtypeStruct((nb, D_MODEL), _F32),
        jax.ShapeDtypeStruct((nb, D_STATE), _F32),
        jax.ShapeDtypeStruct((nb, D_STATE), _F32),
    )
    out_specs = [tile] + [pl.BlockSpec(s.shape, lambda i: (0, 0)) for s in out_shape[1:]]
    return pl.pallas_call(
        functools.partial(_mixer_kernel, nb, tt),
        out_shape=out_shape,
        grid=(n_rows // m,),
        in_specs=in_specs,
        out_specs=out_specs,
        scratch_shapes=[
            pltpu.VMEM((hist + m, D_MODEL), _F32),
            pltpu.VMEM((m, D_MODEL), _F32),
            pltpu.VMEM((m, D_MODEL), _F32),
            pltpu.VMEM((m, D_STATE), _F32),
            pltpu.VMEM((m, D_STATE), _F32),
        ],
        compiler_params=pltpu.CompilerParams(
            dimension_semantics=("arbitrary",), vmem_limit_bytes=VMEM_LIMIT_BYTES),
        name=f"mixer_nb{nb}",
    )(*operands)


def _run_ffn(x_rows, nb, tt, weights, conv0):
    n_rows = x_rows.shape[0]
    m = nb * tt
    hist = (FFN_CONV - 1) * nb
    tile = pl.BlockSpec((m, D_MODEL), lambda i: (i, 0))
    operands = (x_rows,) + tuple(weights) + (conv0,)
    in_specs = [tile] + [_resident(a.shape) for a in operands[1:]]
    out_shape = (
        jax.ShapeDtypeStruct((n_rows, D_MODEL), _F32),
        jax.ShapeDtypeStruct((hist, D_FF), _F32),
    )
    out_specs = [tile, pl.BlockSpec((hist, D_FF), lambda i: (0, 0))]
    return pl.pallas_call(
        functools.partial(_ffn_kernel, nb, tt),
        out_shape=out_shape,
        grid=(n_rows // m,),
        in_specs=in_specs,
        out_specs=out_specs,
        scratch_shapes=[pltpu.VMEM((hist + m, D_FF), _F32)],
        compiler_params=pltpu.CompilerParams(
            dimension_semantics=("arbitrary",), vmem_limit_bytes=VMEM_LIMIT_BYTES),
        name=f"ffn_nb{nb}",
    )(*operands)


def _block_diag(blocks, n_per_tile):
    n, r, c = blocks.shape
    tiles = n // n_per_tile
    eye = jnp.eye(n_per_tile, dtype=blocks.dtype)
    b = blocks.reshape(tiles, n_per_tile, r, c)
    return jnp.einsum("tgrc,gh->tgrhc", b, eye).reshape(tiles, n_per_tile * r, n_per_tile * c)


def _s5_discretise(a_re, a_im, log_dt, b_re, b_im):
    dt = jnp.exp(log_dt)[:, None]
    mag = jnp.exp(a_re * dt)
    ab_re = mag * jnp.cos(a_im * dt)
    ab_im = mag * jnp.sin(a_im * dt)
    nr = ab_re - 1.0
    ni = ab_im
    den = a_re * a_re + a_im * a_im
    coef_re = (nr * a_re + ni * a_im) / den
    coef_im = (ni * a_re - nr * a_im) / den
    bb_re = coef_re[..., None] * b_re - coef_im[..., None] * b_im
    bb_im = coef_re[..., None] * b_im + coef_im[..., None] * b_re
    return ab_re, ab_im, bb_re, bb_im


def _time_major(x):
    b, l, c = x.shape
    return jnp.transpose(x, (1, 0, 2)).reshape(l * b, c)


def _batch_major(rows, b):
    n, c = rows.shape
    return jnp.transpose(rows.reshape(n // b, b, c), (1, 0, 2))


def _layer(x, lru_conv0, lru_h0, s5_re0, s5_im0, ffn_conv0, tt, mixer_w, ffn_w):
    nb, seq, _ = x.shape
    x_rows = _time_major(x)
    states = (_time_major(lru_conv0), lru_h0,
              s5_re0.reshape(nb, D_STATE), s5_im0.reshape(nb, D_STATE))
    x1, conv_out, h_out, sre, sim = _run_mixer(x_rows, nb, tt, mixer_w, states)
    y, fconv_out = _run_ffn(x1, nb, tt, ffn_w, _time_major(ffn_conv0))
    return (_batch_major(y, nb), _batch_major(conv_out, nb), h_out,
            sre.reshape(nb, S5_GROUPS, S5_STATE), sim.reshape(nb, S5_GROUPS, S5_STATE),
            _batch_major(fconv_out, nb))


def kernel(x_prompt, x_sample, state_lru_conv, state_lru_h, state_s5_re, state_s5_im, state_ffn_conv, w_in, lru_conv_w, lru_conv_b, lru_wa, lru_ba, lru_wx, lru_bx, lru_lambda, s5_a_re, s5_a_im, s5_log_dt, s5_b_re, s5_b_im, s5_c_re, s5_c_im, s5_d, w_glu, w_out, ln1_g, ln1_b, w_up, ffn_conv_w, ffn_conv_b, w_down, ln2_g, ln2_b):
    n_p = x_prompt.shape[0]
    xp, xs = x_prompt, x_sample
    outs_p, outs_s = [], []
    for l in range(DEPTH):
        row = lambda v: v[l].reshape(1, -1)
        heads_per_tile = MXU_TILE // LRU_HEAD_DIM
        wg = jnp.concatenate([_block_diag(lru_wa[l], heads_per_tile),
                              _block_diag(lru_wx[l], heads_per_tile)], axis=2).astype(_BF16)
        ab_re, ab_im, bb_re, bb_im = _s5_discretise(s5_a_re[l], s5_a_im[l], s5_log_dt[l], s5_b_re[l], s5_b_im[l])
        groups_per_block = S5_IN_BLOCK // S5_GROUP
        to_in = lambda bb: _block_diag(jnp.swapaxes(bb, 1, 2), groups_per_block)
        to_out = lambda cc: _block_diag(jnp.swapaxes(cc, 1, 2), groups_per_block)
        wb = jnp.concatenate([to_in(bb_re), to_in(bb_im)], axis=2).astype(_BF16)
        mixer_w = (
            w_in[l].astype(_BF16), lru_conv_w[l], row(lru_conv_b), wg, row(lru_ba), row(lru_bx), row(lru_lambda),
            wb, ab_re.reshape(1, D_STATE), ab_im.reshape(1, D_STATE),
            to_out(s5_c_re[l]).astype(_BF16), to_out(s5_c_im[l]).astype(_BF16), row(s5_d),
            w_glu[l].astype(_BF16), w_out[l].astype(_BF16), row(ln1_g), row(ln1_b),
        )
        ffn_w = (w_up[l].astype(_BF16), ffn_conv_w[l], row(ffn_conv_b), w_down[l].astype(_BF16),
                 row(ln2_g), row(ln2_b))

        zc = jnp.zeros((n_p, LRU_CONV - 1, D_MODEL), _F32)
        zh = jnp.zeros((n_p, D_MODEL), _F32)
        zs = jnp.zeros((n_p, S5_GROUPS, S5_STATE), _F32)
        zf = jnp.zeros((n_p, FFN_CONV - 1, D_FF), _F32)
        xp, *st_p = _layer(xp, zc, zh, zs, zs, zf, 32, mixer_w, ffn_w)
        outs_p.append(st_p)
        xs, *st_s = _layer(xs, state_lru_conv[l], state_lru_h[l], state_s5_re[l], state_s5_im[l],
                           state_ffn_conv[l], xs.shape[1], mixer_w, ffn_w)
        outs_s.append(st_s)
    stack = lambda outs, j: jnp.stack([o[j] for o in outs])
    return (xp, xs,
            *(stack(outs_p, j) for j in range(5)),
            *(stack(outs_s, j) for j in range(5)))
```

```python
import functools
import math

import jax
import jax.numpy as jnp
from jax.experimental import pallas as pl
from jax.experimental.pallas import tpu as pltpu

D_MODEL = 1024
LRU_HEADS = 16
LRU_HEAD_DIM = D_MODEL // LRU_HEADS
LRU_CONV = 4
LRU_C = 8.0
S5_GROUP = 16
D_S5 = D_MODEL // 2
S5_GROUPS = D_S5 // S5_GROUP
S5_STATE = 64
D_STATE = S5_GROUPS * S5_STATE
D_FF = 3 * D_MODEL
FFN_CONV = 3
DEPTH = 1
ALPHA = (2.0 * DEPTH) ** 0.25
LN_EPS = 1e-5

MXU_TILE = 256
SUBLANES = 8
LANES = 128
LANE_BLOCKS = D_MODEL // LANES
GATE_BLOCKS = D_MODEL // MXU_TILE
S5_IN_BLOCK = 128
S5_BLOCKS = D_S5 // S5_IN_BLOCK
S5_STATE_BLOCK = D_STATE // S5_BLOCKS
VMEM_LIMIT_BYTES = 56 * 1024 * 1024

_BF16 = jnp.bfloat16
_F32 = jnp.float32


def _dot(a, b):
    return jnp.dot(a, b, preferred_element_type=_F32)


def _layer_norm(y, g, b):
    mu = jnp.mean(y, axis=-1, keepdims=True)
    yc = y - mu
    var = jnp.mean(yc * yc, axis=-1, keepdims=True)
    return yc * jax.lax.rsqrt(var + LN_EPS) * g + b


def _mixer_kernel(nb, tt, batch_major,
                  x_ref, w_in_ref, cw_ref, cb_ref, wg_ref, ba_ref, bx_ref, lam_ref,
                  wb_ref, are_ref, aim_ref, wcr_ref, wci_ref, d_ref, wglu_ref, wout_ref,
                  g1_ref, b1_ref, conv0_ref, h0_ref, sre0_ref, sim0_ref,
                  x1_ref, conv_out_ref, h_ref, sre_ref, sim_ref,
                  xbuf, abuf, hbuf, bure, buim, *maybe_xt):
    m = nb * tt
    hist = (LRU_CONV - 1) * nb

    @pl.when(pl.program_id(0) == 0)
    def _():
        xbuf[0:hist, :] = conv0_ref[...]
        h_ref[...] = h0_ref[...]
        sre_ref[...] = sre0_ref[...]
        sim_ref[...] = sim0_ref[...]

    if batch_major:
        (xt,) = maybe_xt
        for b in range(nb):
            for c in range(LANE_BLOCKS):
                xt[c, pl.ds(b, tt, stride=nb), :] = x_ref[b, :, c * LANES:(c + 1) * LANES]
        load_x = lambda: jnp.concatenate([xt[c] for c in range(LANE_BLOCKS)], axis=1)
    else:
        load_x = lambda: x_ref[...]
    xb = load_x().astype(_BF16)

    xbuf[hist:hist + m, :] = _dot(xb, w_in_ref[:, 0:D_MODEL])
    xc = cb_ref[...] + xbuf[0:m, :] * cw_ref[0:1, :]
    for k in range(1, LRU_CONV):
        xc = xc + xbuf[k * nb:k * nb + m, :] * cw_ref[k:k + 1, :]
    new_hist = xbuf[m:m + hist, :]
    xbuf[0:hist, :] = new_hist
    conv_out_ref[...] = new_hist

    xcb = xc.astype(_BF16)
    lam = lam_ref[...]
    c_lam = LRU_C * (jnp.minimum(lam, 0.0) - jnp.log1p(jnp.exp(-jnp.abs(lam))))
    for k in range(GATE_BLOCKS):
        blk = slice(k * MXU_TILE, (k + 1) * MXU_TILE)
        g = _dot(xcb[:, blk], wg_ref[k])
        r = jax.nn.sigmoid(g[:, 0:MXU_TILE] + ba_ref[:, blk])
        ig = jax.nn.sigmoid(g[:, MXU_TILE:] + bx_ref[:, blk])
        log_a = r * c_lam[:, blk]
        abuf[:, blk] = jnp.exp(log_a)
        th = jnp.tanh(log_a)
        mult = jnp.sqrt(jnp.maximum(-2.0 * th / (1.0 - th), 0.0))
        hbuf[:, blk] = mult * (ig * xc[:, blk])

    h = h_ref[...]
    for t in range(tt):
        rows = slice(t * nb, (t + 1) * nb)
        h = abuf[rows, :] * h + hbuf[rows, :]
        hbuf[rows, :] = h
    h_ref[...] = h

    u = _dot(xb, w_in_ref[:, D_MODEL:D_MODEL + D_S5])
    ub = u.astype(_BF16)
    for k in range(S5_BLOCKS):
        bu = _dot(ub[:, k * S5_IN_BLOCK:(k + 1) * S5_IN_BLOCK], wb_ref[k])
        cols = slice(k * S5_STATE_BLOCK, (k + 1) * S5_STATE_BLOCK)
        bure[:, cols] = bu[:, 0:S5_STATE_BLOCK]
        buim[:, cols] = bu[:, S5_STATE_BLOCK:]

    for k in range(S5_BLOCKS):
        cols = slice(k * S5_STATE_BLOCK, (k + 1) * S5_STATE_BLOCK)
        ar = jnp.broadcast_to(are_ref[:, cols], (nb, S5_STATE_BLOCK))
        ai = jnp.broadcast_to(aim_ref[:, cols], (nb, S5_STATE_BLOCK))
        hr = sre_ref[:, cols]
        hi = sim_ref[:, cols]
        for t in range(tt):
            rows = slice(t * nb, (t + 1) * nb)
            hr_new = ar * hr - ai * hi + bure[rows, cols]
            hi_new = ar * hi + ai * hr + buim[rows, cols]
            hr, hi = hr_new, hi_new
            bure[rows, cols] = hr
            buim[rows, cols] = hi
        sre_ref[:, cols] = hr
        sim_ref[:, cols] = hi

    ys = []
    for k in range(S5_BLOCKS):
        cols = slice(k * S5_STATE_BLOCK, (k + 1) * S5_STATE_BLOCK)
        ys.append(_dot(bure[:, cols].astype(_BF16), wcr_ref[k])
                  - _dot(buim[:, cols].astype(_BF16), wci_ref[k]))
    y = jnp.concatenate(ys, axis=1) + d_ref[...] * u
    z = jax.nn.gelu(y).astype(_BF16)
    glu = _dot(z, wglu_ref[...])
    s5_out = glu[:, 0:D_MODEL] * jax.nn.sigmoid(glu[:, D_MODEL:])

    g_lru = jax.nn.sigmoid(_dot(xb, w_in_ref[:, D_MODEL + D_S5:2 * D_MODEL + D_S5]))
    g_s5 = jax.nn.sigmoid(_dot(xb, w_in_ref[:, 2 * D_MODEL + D_S5:]))
    merged = (g_lru * hbuf[...] + g_s5 * s5_out).astype(_BF16)
    mix = _dot(merged, wout_ref[...])
    x1_ref[...] = _layer_norm(ALPHA * load_x() + mix, g1_ref[...], b1_ref[...])


def _ffn_kernel(nb, tt, batch_major,
                x_ref, wup_ref, cw_ref, cb_ref, wdown_ref, g2_ref, b2_ref, conv0_ref,
                out_ref, conv_out_ref,
                abuf, *maybe_yt):
    m = nb * tt
    hist = (FFN_CONV - 1) * nb

    @pl.when(pl.program_id(0) == 0)
    def _():
        abuf[0:hist, :] = conv0_ref[...]

    x = x_ref[...]
    xb = x.astype(_BF16)
    abuf[hist:hist + m, :] = _dot(xb, wup_ref[:, 0:D_FF])
    f = None
    for c in range(D_FF // D_MODEL):
        cols = slice(c * D_MODEL, (c + 1) * D_MODEL)
        ac = cb_ref[:, cols] + abuf[0:m, cols] * cw_ref[0:1, cols]
        for k in range(1, FFN_CONV):
            ac = ac + abuf[k * nb:k * nb + m, cols] * cw_ref[k:k + 1, cols]
        gate = _dot(xb, wup_ref[:, D_FF + c * D_MODEL:D_FF + (c + 1) * D_MODEL])
        hmid = (jax.nn.gelu(ac) * gate).astype(_BF16)
        part = _dot(hmid, wdown_ref[cols, :])
        f = part if f is None else f + part
    new_hist = abuf[m:m + hist, :]
    abuf[0:hist, :] = new_hist
    conv_out_ref[...] = new_hist
    y = _layer_norm(ALPHA * x + f, g2_ref[...], b2_ref[...])
    if batch_major:
        (yt,) = maybe_yt
        for c in range(LANE_BLOCKS):
            yt[c] = y[:, c * LANES:(c + 1) * LANES]
        for b in range(nb):
            for c in range(LANE_BLOCKS):
                out_ref[b, :, c * LANES:(c + 1) * LANES] = yt[c, pl.ds(b, tt, stride=nb), :]
    else:
        out_ref[...] = y


def _resident(shape):
    zeros = (0,) * len(shape)
    return pl.BlockSpec(shape, lambda i: zeros, pipeline_mode=pl.Buffered(1))


def _run_mixer(x, nb, tt, weights, states, batch_major):
    m = nb * tt
    hist = (LRU_CONV - 1) * nb
    tile = pl.BlockSpec((m, D_MODEL), lambda i: (i, 0))
    if batch_major:
        n_rows = nb * x.shape[1]
        x_spec = pl.BlockSpec((nb, tt, D_MODEL), lambda i: (0, i, 0))
    else:
        n_rows = x.shape[0]
        x_spec = tile
    operands = (x,) + tuple(weights) + tuple(states)
    in_specs = [x_spec] + [_resident(a.shape) for a in operands[1:]]
    out_shape = (
        jax.ShapeDtypeStruct((n_rows, D_MODEL), _F32),
        jax.ShapeDtypeStruct((hist, D_MODEL), _F32),
        jax.ShapeDtypeStruct((nb, D_MODEL), _F32),
        jax.ShapeDtypeStruct((nb, D_STATE), _F32),
        jax.ShapeDtypeStruct((nb, D_STATE), _F32),
    )
    out_specs = [tile] + [pl.BlockSpec(s.shape, lambda i: (0, 0)) for s in out_shape[1:]]
    scratch_shapes = [
        pltpu.VMEM((hist + m, D_MODEL), _F32),
        pltpu.VMEM((m, D_MODEL), _F32),
        pltpu.VMEM((m, D_MODEL), _F32),
        pltpu.VMEM((m, D_STATE), _F32),
        pltpu.VMEM((m, D_STATE), _F32),
    ]
    if batch_major:
        scratch_shapes.append(pltpu.VMEM((LANE_BLOCKS, m, LANES), _F32))
    return pl.pallas_call(
        functools.partial(_mixer_kernel, nb, tt, batch_major),
        out_shape=out_shape,
        grid=(n_rows // m,),
        in_specs=in_specs,
        out_specs=out_specs,
        scratch_shapes=scratch_shapes,
        compiler_params=pltpu.CompilerParams(
            dimension_semantics=("arbitrary",), vmem_limit_bytes=VMEM_LIMIT_BYTES),
        name=f"mixer_nb{nb}",
    )(*operands)


def _run_ffn(x_rows, nb, tt, weights, conv0, batch_major):
    n_rows = x_rows.shape[0]
    m = nb * tt
    hist = (FFN_CONV - 1) * nb
    tile = pl.BlockSpec((m, D_MODEL), lambda i: (i, 0))
    operands = (x_rows,) + tuple(weights) + (conv0,)
    in_specs = [tile] + [_resident(a.shape) for a in operands[1:]]
    if batch_major:
        y_shape = jax.ShapeDtypeStruct((nb, n_rows // nb, D_MODEL), _F32)
        y_spec = pl.BlockSpec((nb, tt, D_MODEL), lambda i: (0, i, 0))
    else:
        y_shape = jax.ShapeDtypeStruct((n_rows, D_MODEL), _F32)
        y_spec = tile
    out_shape = (y_shape, jax.ShapeDtypeStruct((hist, D_FF), _F32))
    out_specs = [y_spec, pl.BlockSpec((hist, D_FF), lambda i: (0, 0))]
    scratch_shapes = [pltpu.VMEM((hist + m, D_FF), _F32)]
    if batch_major:
        scratch_shapes.append(pltpu.VMEM((LANE_BLOCKS, m, LANES), _F32))
    return pl.pallas_call(
        functools.partial(_ffn_kernel, nb, tt, batch_major),
        out_shape=out_shape,
        grid=(n_rows // m,),
        in_specs=in_specs,
        out_specs=out_specs,
        scratch_shapes=scratch_shapes,
        compiler_params=pltpu.CompilerParams(
            dimension_semantics=("arbitrary",), vmem_limit_bytes=VMEM_LIMIT_BYTES),
        name=f"ffn_nb{nb}",
    )(*operands)


def _block_diag(blocks, n_per_tile):
    n, r, c = blocks.shape
    tiles = n // n_per_tile
    eye = jnp.eye(n_per_tile, dtype=blocks.dtype)
    b = blocks.reshape(tiles, n_per_tile, r, c)
    return jnp.einsum("tgrc,gh->tgrhc", b, eye).reshape(tiles, n_per_tile * r, n_per_tile * c)


def _s5_discretise(a_re, a_im, log_dt, b_re, b_im):
    dt = jnp.exp(log_dt)[:, None]
    mag = jnp.exp(a_re * dt)
    ab_re = mag * jnp.cos(a_im * dt)
    ab_im = mag * jnp.sin(a_im * dt)
    nr = ab_re - 1.0
    ni = ab_im
    den = a_re * a_re + a_im * a_im
    coef_re = (nr * a_re + ni * a_im) / den
    coef_im = (ni * a_re - nr * a_im) / den
    bb_re = coef_re[..., None] * b_re - coef_im[..., None] * b_im
    bb_im = coef_re[..., None] * b_im + coef_im[..., None] * b_re
    return ab_re, ab_im, bb_re, bb_im


def _time_major(x):
    b, l, c = x.shape
    return jnp.transpose(x, (1, 0, 2)).reshape(l * b, c)


def _batch_major(rows, b):
    n, c = rows.shape
    return jnp.transpose(rows.reshape(n // b, b, c), (1, 0, 2))


def _layer(x, lru_conv0, lru_h0, s5_re0, s5_im0, ffn_conv0, tt, mixer_w, ffn_w):
    nb = x.shape[0]
    batch_major = tt % SUBLANES == 0
    states = (_time_major(lru_conv0), lru_h0,
              s5_re0.reshape(nb, D_STATE), s5_im0.reshape(nb, D_STATE))
    x1, conv_out, h_out, sre, sim = _run_mixer(
        x if batch_major else _time_major(x), nb, tt, mixer_w, states, batch_major)
    y, fconv_out = _run_ffn(x1, nb, tt, ffn_w, _time_major(ffn_conv0), batch_major)
    return (y if batch_major else _batch_major(y, nb), _batch_major(conv_out, nb), h_out,
            sre.reshape(nb, S5_GROUPS, S5_STATE), sim.reshape(nb, S5_GROUPS, S5_STATE),
            _batch_major(fconv_out, nb))


def kernel(x_prompt, x_sample, state_lru_conv, state_lru_h, state_s5_re, state_s5_im, state_ffn_conv, w_in, lru_conv_w, lru_conv_b, lru_wa, lru_ba, lru_wx, lru_bx, lru_lambda, s5_a_re, s5_a_im, s5_log_dt, s5_b_re, s5_b_im, s5_c_re, s5_c_im, s5_d, w_glu, w_out, ln1_g, ln1_b, w_up, ffn_conv_w, ffn_conv_b, w_down, ln2_g, ln2_b):
    n_p = x_prompt.shape[0]
    xp, xs = x_prompt, x_sample
    outs_p, outs_s = [], []
    for l in range(DEPTH):
        row = lambda v: v[l].reshape(1, -1)
        heads_per_tile = MXU_TILE // LRU_HEAD_DIM
        wg = jnp.concatenate([_block_diag(lru_wa[l], heads_per_tile),
                              _block_diag(lru_wx[l], heads_per_tile)], axis=2).astype(_BF16)
        ab_re, ab_im, bb_re, bb_im = _s5_discretise(s5_a_re[l], s5_a_im[l], s5_log_dt[l], s5_b_re[l], s5_b_im[l])
        groups_per_block = S5_IN_BLOCK // S5_GROUP
        to_in = lambda bb: _block_diag(jnp.swapaxes(bb, 1, 2), groups_per_block)
        to_out = lambda cc: _block_diag(jnp.swapaxes(cc, 1, 2), groups_per_block)
        wb = jnp.concatenate([to_in(bb_re), to_in(bb_im)], axis=2).astype(_BF16)
        mixer_w = (
            w_in[l].astype(_BF16), lru_conv_w[l], row(lru_conv_b), wg, row(lru_ba), row(lru_bx), row(lru_lambda),
            wb, ab_re.reshape(1, D_STATE), ab_im.reshape(1, D_STATE),
            to_out(s5_c_re[l]).astype(_BF16), to_out(s5_c_im[l]).astype(_BF16), row(s5_d),
            w_glu[l].astype(_BF16), w_out[l].astype(_BF16), row(ln1_g), row(ln1_b),
        )
        ffn_w = (w_up[l].astype(_BF16), ffn_conv_w[l], row(ffn_conv_b), w_down[l].astype(_BF16),
                 row(ln2_g), row(ln2_b))

        zc = jnp.zeros((n_p, LRU_CONV - 1, D_MODEL), _F32)
        zh = jnp.zeros((n_p, D_MODEL), _F32)
        zs = jnp.zeros((n_p, S5_GROUPS, S5_STATE), _F32)
        zf = jnp.zeros((n_p, FFN_CONV - 1, D_FF), _F32)
        xp, *st_p = _layer(xp, zc, zh, zs, zs, zf, 32, mixer_w, ffn_w)
        outs_p.append(st_p)
        xs, *st_s = _layer(xs, state_lru_conv[l], state_lru_h[l], state_s5_re[l], state_s5_im[l],
                           state_ffn_conv[l], xs.shape[1], mixer_w, ffn_w)
        outs_s.append(st_s)
    stack = lambda outs, j: jnp.stack([o[j] for o in outs])
    return (xp, xs,
            *(stack(outs_p, j) for j in range(5)),
            *(stack(outs_s, j) for j in range(5)))
```

```python
import functools
import math

import jax
import jax.numpy as jnp
from jax.experimental import pallas as pl
from jax.experimental.pallas import tpu as pltpu

D_MODEL = 1024
LRU_HEADS = 16
LRU_HEAD_DIM = D_MODEL // LRU_HEADS
LRU_CONV = 4
LRU_C = 8.0
S5_GROUP = 16
D_S5 = D_MODEL // 2
S5_GROUPS = D_S5 // S5_GROUP
S5_STATE = 64
D_STATE = S5_GROUPS * S5_STATE
D_FF = 3 * D_MODEL
FFN_CONV = 3
DEPTH = 1
ALPHA = (2.0 * DEPTH) ** 0.25
LN_EPS = 1e-5
LOG2_E = math.log2(math.e)
RSQRT_FLOOR = 1e-36

MXU_TILE = 256
SUBLANES = 8
LANES = 128
LANE_BLOCKS = D_MODEL // LANES
GATE_BLOCKS = D_MODEL // MXU_TILE
S5_IN_BLOCK = 128
S5_BLOCKS = D_S5 // S5_IN_BLOCK
S5_STATE_BLOCK = D_STATE // S5_BLOCKS
PROMPT_TILE_STEPS = 64
VMEM_LIMIT_BYTES = 56 * 1024 * 1024

_BF16 = jnp.bfloat16
_F32 = jnp.float32


def _dot(a, b):
    return jnp.dot(a, b, preferred_element_type=_F32)


def _sigmoid_of_twice(half_z):
    return 0.5 * jnp.tanh(half_z) + 0.5


def _layer_norm(y, g, b):
    mu = jnp.mean(y, axis=-1, keepdims=True)
    yc = y - mu
    var = jnp.mean(yc * yc, axis=-1, keepdims=True)
    return yc * jax.lax.rsqrt(var + LN_EPS) * g + b


def _mixer_kernel(nb, tt, batch_major,
                  x_ref, w_in_ref, cw_ref, cb_ref, wg_ref, ba_ref, bx_ref, lam_ref,
                  wb_ref, are_ref, aim_ref, wcr_ref, wci_ref, d_ref, wglu_ref, wout_ref,
                  g1_ref, b1_ref, conv0_ref, h0_ref, sre0_ref, sim0_ref,
                  x1_ref, conv_out_ref, h_ref, sre_ref, sim_ref,
                  xbuf, abuf, hbuf, bure, buim, *maybe_xt):
    m = nb * tt
    hist = (LRU_CONV - 1) * nb

    @pl.when(pl.program_id(0) == 0)
    def _():
        xbuf[0:hist, :] = conv0_ref[...]
        h_ref[...] = h0_ref[...]
        sre_ref[...] = sre0_ref[...]
        sim_ref[...] = sim0_ref[...]

    if batch_major:
        (xt,) = maybe_xt
        for b in range(nb):
            for c in range(LANE_BLOCKS):
                xt[c, pl.ds(b, tt, stride=nb), :] = x_ref[b, :, c * LANES:(c + 1) * LANES]
        load_x = lambda: jnp.concatenate([xt[c] for c in range(LANE_BLOCKS)], axis=1)
    else:
        load_x = lambda: x_ref[...]
    xb = load_x().astype(_BF16)

    xbuf[hist:hist + m, :] = _dot(xb, w_in_ref[:, 0:D_MODEL])
    xc = cb_ref[...] + xbuf[0:m, :] * cw_ref[0:1, :]
    for k in range(1, LRU_CONV):
        xc = xc + xbuf[k * nb:k * nb + m, :] * cw_ref[k:k + 1, :]
    new_hist = xbuf[m:m + hist, :]
    xbuf[0:hist, :] = new_hist
    conv_out_ref[...] = new_hist

    xcb = xc.astype(_BF16)
    lam = lam_ref[...]
    c_lam = (0.5 * LRU_C * LOG2_E) * (jnp.minimum(lam, 0.0) - jnp.log1p(jnp.exp(-jnp.abs(lam))))
    for k in range(GATE_BLOCKS):
        blk = slice(k * MXU_TILE, (k + 1) * MXU_TILE)
        half_g = _dot(xcb[:, blk], wg_ref[k])
        tanh_r = jnp.tanh(half_g[:, 0:MXU_TILE] + ba_ref[:, blk])
        ig = _sigmoid_of_twice(half_g[:, MXU_TILE:] + bx_ref[:, blk])
        a = jnp.exp2(tanh_r * c_lam[:, blk] + c_lam[:, blk])
        abuf[:, blk] = a
        one_minus_a2 = jnp.maximum((1.0 - a) * (1.0 + a), 0.0)
        mult = one_minus_a2 * jax.lax.rsqrt(jnp.maximum(one_minus_a2, RSQRT_FLOOR))
        hbuf[:, blk] = mult * (ig * xc[:, blk])

    h = h_ref[...]
    for t in range(tt):
        rows = slice(t * nb, (t + 1) * nb)
        h = abuf[rows, :] * h + hbuf[rows, :]
        hbuf[rows, :] = h
    h_ref[...] = h

    u = _dot(xb, w_in_ref[:, D_MODEL:D_MODEL + D_S5])
    ub = u.astype(_BF16)
    for k in range(S5_BLOCKS):
        bu = _dot(ub[:, k * S5_IN_BLOCK:(k + 1) * S5_IN_BLOCK], wb_ref[k])
        cols = slice(k * S5_STATE_BLOCK, (k + 1) * S5_STATE_BLOCK)
        bure[:, cols] = bu[:, 0:S5_STATE_BLOCK]
        buim[:, cols] = bu[:, S5_STATE_BLOCK:]

    for k in range(S5_BLOCKS):
        cols = slice(k * S5_STATE_BLOCK, (k + 1) * S5_STATE_BLOCK)
        ar = jnp.broadcast_to(are_ref[:, cols], (nb, S5_STATE_BLOCK))
        ai = jnp.broadcast_to(aim_ref[:, cols], (nb, S5_STATE_BLOCK))
        hr = sre_ref[:, cols]
        hi = sim_ref[:, cols]
        for t in range(tt):
            rows = slice(t * nb, (t + 1) * nb)
            hr_new = ar * hr - ai * hi + bure[rows, cols]
            hi_new = ar * hi + ai * hr + buim[rows, cols]
            hr, hi = hr_new, hi_new
            bure[rows, cols] = hr
            buim[rows, cols] = hi
        sre_ref[:, cols] = hr
        sim_ref[:, cols] = hi

    ys = []
    for k in range(S5_BLOCKS):
        cols = slice(k * S5_STATE_BLOCK, (k + 1) * S5_STATE_BLOCK)
        ys.append(_dot(bure[:, cols].astype(_BF16), wcr_ref[k])
                  - _dot(buim[:, cols].astype(_BF16), wci_ref[k]))
    y = jnp.concatenate(ys, axis=1) + d_ref[...] * u
    z = jax.nn.gelu(y).astype(_BF16)
    glu = _dot(z, wglu_ref[...])
    s5_out = glu[:, 0:D_MODEL] * _sigmoid_of_twice(glu[:, D_MODEL:])

    g_lru = _sigmoid_of_twice(_dot(xb, w_in_ref[:, D_MODEL + D_S5:2 * D_MODEL + D_S5]))
    g_s5 = _sigmoid_of_twice(_dot(xb, w_in_ref[:, 2 * D_MODEL + D_S5:]))
    merged = (g_lru * hbuf[...] + g_s5 * s5_out).astype(_BF16)
    mix = _dot(merged, wout_ref[...])
    x1_ref[...] = _layer_norm(ALPHA * load_x() + mix, g1_ref[...], b1_ref[...])


def _ffn_kernel(nb, tt, batch_major,
                x_ref, wup_ref, cw_ref, cb_ref, wdown_ref, g2_ref, b2_ref, conv0_ref,
                out_ref, conv_out_ref,
                abuf, *maybe_yt):
    m = nb * tt
    hist = (FFN_CONV - 1) * nb

    @pl.when(pl.program_id(0) == 0)
    def _():
        abuf[0:hist, :] = conv0_ref[...]

    x = x_ref[...]
    xb = x.astype(_BF16)
    abuf[hist:hist + m, :] = _dot(xb, wup_ref[:, 0:D_FF])
    f = None
    for c in range(D_FF // D_MODEL):
        cols = slice(c * D_MODEL, (c + 1) * D_MODEL)
        ac = cb_ref[:, cols] + abuf[0:m, cols] * cw_ref[0:1, cols]
        for k in range(1, FFN_CONV):
            ac = ac + abuf[k * nb:k * nb + m, cols] * cw_ref[k:k + 1, cols]
        gate = _dot(xb, wup_ref[:, D_FF + c * D_MODEL:D_FF + (c + 1) * D_MODEL])
        hmid = (jax.nn.gelu(ac) * gate).astype(_BF16)
        part = _dot(hmid, wdown_ref[cols, :])
        f = part if f is None else f + part
    new_hist = abuf[m:m + hist, :]
    abuf[0:hist, :] = new_hist
    conv_out_ref[...] = new_hist
    y = _layer_norm(ALPHA * x + f, g2_ref[...], b2_ref[...])
    if batch_major:
        (yt,) = maybe_yt
        for c in range(LANE_BLOCKS):
            yt[c] = y[:, c * LANES:(c + 1) * LANES]
        for b in range(nb):
            for c in range(LANE_BLOCKS):
                out_ref[b, :, c * LANES:(c + 1) * LANES] = yt[c, pl.ds(b, tt, stride=nb), :]
    else:
        out_ref[...] = y


def _resident(shape):
    zeros = (0,) * len(shape)
    return pl.BlockSpec(shape, lambda i: zeros, pipeline_mode=pl.Buffered(1))


def _run_mixer(x, nb, tt, weights, states, batch_major):
    m = nb * tt
    hist = (LRU_CONV - 1) * nb
    tile = pl.BlockSpec((m, D_MODEL), lambda i: (i, 0))
    if batch_major:
        n_rows = nb * x.shape[1]
        x_spec = pl.BlockSpec((nb, tt, D_MODEL), lambda i: (0, i, 0))
    else:
        n_rows = x.shape[0]
        x_spec = tile
    operands = (x,) + tuple(weights) + tuple(states)
    in_specs = [x_spec] + [_resident(a.shape) for a in operands[1:]]
    out_shape = (
        jax.ShapeDtypeStruct((n_rows, D_MODEL), _F32),
        jax.ShapeDtypeStruct((hist, D_MODEL), _F32),
        jax.ShapeDtypeStruct((nb, D_MODEL), _F32),
        jax.ShapeDtypeStruct((nb, D_STATE), _F32),
        jax.ShapeDtypeStruct((nb, D_STATE), _F32),
    )
    out_specs = [tile] + [pl.BlockSpec(s.shape, lambda i: (0, 0)) for s in out_shape[1:]]
    scratch_shapes = [
        pltpu.VMEM((hist + m, D_MODEL), _F32),
        pltpu.VMEM((m, D_MODEL), _F32),
        pltpu.VMEM((m, D_MODEL), _F32),
        pltpu.VMEM((m, D_STATE), _F32),
        pltpu.VMEM((m, D_STATE), _F32),
    ]
    if batch_major:
        scratch_shapes.append(pltpu.VMEM((LANE_BLOCKS, m, LANES), _F32))
    return pl.pallas_call(
        functools.partial(_mixer_kernel, nb, tt, batch_major),
        out_shape=out_shape,
        grid=(n_rows // m,),
        in_specs=in_specs,
        out_specs=out_specs,
        scratch_shapes=scratch_shapes,
        compiler_params=pltpu.CompilerParams(
            dimension_semantics=("arbitrary",), vmem_limit_bytes=VMEM_LIMIT_BYTES),
        name=f"mixer_nb{nb}",
    )(*operands)


def _run_ffn(x_rows, nb, tt, weights, conv0, batch_major):
    n_rows = x_rows.shape[0]
    m = nb * tt
    hist = (FFN_CONV - 1) * nb
    tile = pl.BlockSpec((m, D_MODEL), lambda i: (i, 0))
    operands = (x_rows,) + tuple(weights) + (conv0,)
    in_specs = [tile] + [_resident(a.shape) for a in operands[1:]]
    if batch_major:
        y_shape = jax.ShapeDtypeStruct((nb, n_rows // nb, D_MODEL), _F32)
        y_spec = pl.BlockSpec((nb, tt, D_MODEL), lambda i: (0, i, 0))
    else:
        y_shape = jax.ShapeDtypeStruct((n_rows, D_MODEL), _F32)
        y_spec = tile
    out_shape = (y_shape, jax.ShapeDtypeStruct((hist, D_FF), _F32))
    out_specs = [y_spec, pl.BlockSpec((hist, D_FF), lambda i: (0, 0))]
    scratch_shapes = [pltpu.VMEM((hist + m, D_FF), _F32)]
    if batch_major:
        scratch_shapes.append(pltpu.VMEM((LANE_BLOCKS, m, LANES), _F32))
    return pl.pallas_call(
        functools.partial(_ffn_kernel, nb, tt, batch_major),
        out_shape=out_shape,
        grid=(n_rows // m,),
        in_specs=in_specs,
        out_specs=out_specs,
        scratch_shapes=scratch_shapes,
        compiler_params=pltpu.CompilerParams(
            dimension_semantics=("arbitrary",), vmem_limit_bytes=VMEM_LIMIT_BYTES),
        name=f"ffn_nb{nb}",
    )(*operands)


def _block_diag(blocks, n_per_tile):
    n, r, c = blocks.shape
    tiles = n // n_per_tile
    eye = jnp.eye(n_per_tile, dtype=blocks.dtype)
    b = blocks.reshape(tiles, n_per_tile, r, c)
    return jnp.einsum("tgrc,gh->tgrhc", b, eye).reshape(tiles, n_per_tile * r, n_per_tile * c)


def _s5_discretise(a_re, a_im, log_dt, b_re, b_im):
    dt = jnp.exp(log_dt)[:, None]
    mag = jnp.exp(a_re * dt)
    ab_re = mag * jnp.cos(a_im * dt)
    ab_im = mag * jnp.sin(a_im * dt)
    nr = ab_re - 1.0
    ni = ab_im
    den = a_re * a_re + a_im * a_im
    coef_re = (nr * a_re + ni * a_im) / den
    coef_im = (ni * a_re - nr * a_im) / den
    bb_re = coef_re[..., None] * b_re - coef_im[..., None] * b_im
    bb_im = coef_re[..., None] * b_im + coef_im[..., None] * b_re
    return ab_re, ab_im, bb_re, bb_im


def _time_major(x):
    b, l, c = x.shape
    return jnp.transpose(x, (1, 0, 2)).reshape(l * b, c)


def _batch_major(rows, b):
    n, c = rows.shape
    return jnp.transpose(rows.reshape(n // b, b, c), (1, 0, 2))


def _layer(x, lru_conv0, lru_h0, s5_re0, s5_im0, ffn_conv0, tt, mixer_w, ffn_w):
    nb = x.shape[0]
    batch_major = tt % SUBLANES == 0
    states = (_time_major(lru_conv0), lru_h0,
              s5_re0.reshape(nb, D_STATE), s5_im0.reshape(nb, D_STATE))
    x1, conv_out, h_out, sre, sim = _run_mixer(
        x if batch_major else _time_major(x), nb, tt, mixer_w, states, batch_major)
    y, fconv_out = _run_ffn(x1, nb, tt, ffn_w, _time_major(ffn_conv0), batch_major)
    return (y if batch_major else _batch_major(y, nb), _batch_major(conv_out, nb), h_out,
            sre.reshape(nb, S5_GROUPS, S5_STATE), sim.reshape(nb, S5_GROUPS, S5_STATE),
            _batch_major(fconv_out, nb))


def kernel(x_prompt, x_sample, state_lru_conv, state_lru_h, state_s5_re, state_s5_im, state_ffn_conv, w_in, lru_conv_w, lru_conv_b, lru_wa, lru_ba, lru_wx, lru_bx, lru_lambda, s5_a_re, s5_a_im, s5_log_dt, s5_b_re, s5_b_im, s5_c_re, s5_c_im, s5_d, w_glu, w_out, ln1_g, ln1_b, w_up, ffn_conv_w, ffn_conv_b, w_down, ln2_g, ln2_b):
    n_p = x_prompt.shape[0]
    xp, xs = x_prompt, x_sample
    outs_p, outs_s = [], []
    for l in range(DEPTH):
        row = lambda v: v[l].reshape(1, -1)
        heads_per_tile = MXU_TILE // LRU_HEAD_DIM
        wg = (0.5 * jnp.concatenate([_block_diag(lru_wa[l], heads_per_tile),
                                     _block_diag(lru_wx[l], heads_per_tile)], axis=2)).astype(_BF16)
        w_in_scale = jnp.concatenate([jnp.ones((D_MODEL + D_S5,), _F32), jnp.full((2 * D_MODEL,), 0.5, _F32)])
        w_glu_scale = jnp.concatenate([jnp.ones((D_MODEL,), _F32), jnp.full((D_MODEL,), 0.5, _F32)])
        ab_re, ab_im, bb_re, bb_im = _s5_discretise(s5_a_re[l], s5_a_im[l], s5_log_dt[l], s5_b_re[l], s5_b_im[l])
        groups_per_block = S5_IN_BLOCK // S5_GROUP
        to_in = lambda bb: _block_diag(jnp.swapaxes(bb, 1, 2), groups_per_block)
        to_out = lambda cc: _block_diag(jnp.swapaxes(cc, 1, 2), groups_per_block)
        wb = jnp.concatenate([to_in(bb_re), to_in(bb_im)], axis=2).astype(_BF16)
        mixer_w = (
            (w_in[l] * w_in_scale).astype(_BF16), lru_conv_w[l], row(lru_conv_b), wg,
            0.5 * row(lru_ba), 0.5 * row(lru_bx), row(lru_lambda),
            wb, ab_re.reshape(1, D_STATE), ab_im.reshape(1, D_STATE),
            to_out(s5_c_re[l]).astype(_BF16), to_out(s5_c_im[l]).astype(_BF16), row(s5_d),
            (w_glu[l] * w_glu_scale).astype(_BF16), w_out[l].astype(_BF16), row(ln1_g), row(ln1_b),
        )
        ffn_w = (w_up[l].astype(_BF16), ffn_conv_w[l], row(ffn_conv_b), w_down[l].astype(_BF16),
                 row(ln2_g), row(ln2_b))

        zc = jnp.zeros((n_p, LRU_CONV - 1, D_MODEL), _F32)
        zh = jnp.zeros((n_p, D_MODEL), _F32)
        zs = jnp.zeros((n_p, S5_GROUPS, S5_STATE), _F32)
        zf = jnp.zeros((n_p, FFN_CONV - 1, D_FF), _F32)
        xp, *st_p = _layer(xp, zc, zh, zs, zs, zf, PROMPT_TILE_STEPS, mixer_w, ffn_w)
        outs_p.append(st_p)
        xs, *st_s = _layer(xs, state_lru_conv[l], state_lru_h[l], state_s5_re[l], state_s5_im[l],
                           state_ffn_conv[l], xs.shape[1], mixer_w, ffn_w)
        outs_s.append(st_s)
    stack = lambda outs, j: jnp.stack([o[j] for o in outs])
    return (xp, xs,
            *(stack(outs_p, j) for j in range(5)),
            *(stack(outs_s, j) for j in range(5)))
```

```python
import functools
import math

import jax
import jax.numpy as jnp
from jax.experimental import pallas as pl
from jax.experimental.pallas import tpu as pltpu

D_MODEL = 1024
LRU_HEADS = 16
LRU_HEAD_DIM = D_MODEL // LRU_HEADS
LRU_CONV = 4
LRU_C = 8.0
S5_GROUP = 16
D_S5 = D_MODEL // 2
S5_GROUPS = D_S5 // S5_GROUP
S5_STATE = 64
D_STATE = S5_GROUPS * S5_STATE
D_FF = 3 * D_MODEL
FFN_CONV = 3
DEPTH = 1
ALPHA = (2.0 * DEPTH) ** 0.25
LN_EPS = 1e-5
LOG2_E = math.log2(math.e)
RSQRT_FLOOR = 1e-36

MXU_TILE = 256
SUBLANES = 8
LANES = 128
LANE_BLOCKS = D_MODEL // LANES
GATE_BLOCKS = D_MODEL // MXU_TILE
S5_IN_BLOCK = 128
S5_BLOCKS = D_S5 // S5_IN_BLOCK
S5_STATE_BLOCK = D_STATE // S5_BLOCKS
PROMPT_TILE_STEPS = 64
HEAD_ROWS = 256
VMEM_LIMIT_BYTES = 56 * 1024 * 1024

_BF16 = jnp.bfloat16
_F32 = jnp.float32


def _dot(a, b):
    return jnp.dot(a, b, preferred_element_type=_F32)


def _sigmoid_of_twice(half_z):
    return 0.5 * jnp.tanh(half_z) + 0.5


def _layer_norm(y, g, b):
    mu = jnp.mean(y, axis=-1, keepdims=True)
    yc = y - mu
    var = jnp.mean(yc * yc, axis=-1, keepdims=True)
    return yc * jax.lax.rsqrt(var + LN_EPS) * g + b


def _mixer_kernel(nb, tt, batch_major,
                  x_ref, w_in_ref, cw_ref, cb_ref, wg_ref, ba_ref, bx_ref, lam_ref,
                  wb_ref, are_ref, aim_ref, wcr_ref, wci_ref, d_ref, wglu_ref, wout_ref,
                  conv0_ref, h0_ref, sre0_ref, sim0_ref,
                  y1_ref, conv_out_ref, h_ref, sre_ref, sim_ref,
                  xbuf, abuf, hbuf, bure, buim, *maybe_xt):
    m = nb * tt
    hist = (LRU_CONV - 1) * nb

    @pl.when(pl.program_id(0) == 0)
    def _():
        xbuf[0:hist, :] = conv0_ref[...]
        h_ref[...] = h0_ref[...]
        sre_ref[...] = sre0_ref[...]
        sim_ref[...] = sim0_ref[...]

    if batch_major:
        (xt,) = maybe_xt
        for t0 in range(0, tt, SUBLANES):
            for b in range(nb):
                for c in range(LANE_BLOCKS):
                    xt[c, pl.ds(t0 * nb + b, SUBLANES, stride=nb), :] = (
                        x_ref[b, t0:t0 + SUBLANES, c * LANES:(c + 1) * LANES])
        load_x = lambda: jnp.concatenate([xt[c] for c in range(LANE_BLOCKS)], axis=1)
    else:
        load_x = lambda: x_ref[...]
    xb = load_x().astype(_BF16)

    for r0 in range(0, m, HEAD_ROWS):
        xbuf[hist + r0:hist + r0 + HEAD_ROWS, :] = _dot(xb[r0:r0 + HEAD_ROWS, :], w_in_ref[:, 0:D_MODEL])
    xc = cb_ref[...] + xbuf[0:m, :] * cw_ref[0:1, :]
    for k in range(1, LRU_CONV):
        xc = xc + xbuf[k * nb:k * nb + m, :] * cw_ref[k:k + 1, :]
    new_hist = xbuf[m:m + hist, :]
    xbuf[0:hist, :] = new_hist
    conv_out_ref[...] = new_hist

    xcb = xc.astype(_BF16)
    lam = lam_ref[...]
    c_lam = (0.5 * LRU_C * LOG2_E) * (jnp.minimum(lam, 0.0) - jnp.log1p(jnp.exp(-jnp.abs(lam))))
    for k in range(GATE_BLOCKS):
        blk = slice(k * MXU_TILE, (k + 1) * MXU_TILE)
        half_g = _dot(xcb[:, blk], wg_ref[k])
        tanh_r = jnp.tanh(half_g[:, 0:MXU_TILE] + ba_ref[:, blk])
        ig = _sigmoid_of_twice(half_g[:, MXU_TILE:] + bx_ref[:, blk])
        a = jnp.exp2(tanh_r * c_lam[:, blk] + c_lam[:, blk])
        abuf[:, blk] = a
        one_minus_a2 = jnp.maximum((1.0 - a) * (1.0 + a), 0.0)
        mult = one_minus_a2 * jax.lax.rsqrt(jnp.maximum(one_minus_a2, RSQRT_FLOOR))
        hbuf[:, blk] = mult * (ig * xc[:, blk])

    h = h_ref[...]
    for t in range(tt):
        rows = slice(t * nb, (t + 1) * nb)
        h = abuf[rows, :] * h + hbuf[rows, :]
        hbuf[rows, :] = h
    h_ref[...] = h

    u = _dot(xb, w_in_ref[:, D_MODEL:D_MODEL + D_S5])
    ub = u.astype(_BF16)
    for k in range(S5_BLOCKS):
        bu = _dot(ub[:, k * S5_IN_BLOCK:(k + 1) * S5_IN_BLOCK], wb_ref[k])
        cols = slice(k * S5_STATE_BLOCK, (k + 1) * S5_STATE_BLOCK)
        bure[:, cols] = bu[:, 0:S5_STATE_BLOCK]
        buim[:, cols] = bu[:, S5_STATE_BLOCK:]

    for k in range(S5_BLOCKS):
        cols = slice(k * S5_STATE_BLOCK, (k + 1) * S5_STATE_BLOCK)
        ar = jnp.broadcast_to(are_ref[:, cols], (nb, S5_STATE_BLOCK))
        ai = jnp.broadcast_to(aim_ref[:, cols], (nb, S5_STATE_BLOCK))
        hr = sre_ref[:, cols]
        hi = sim_ref[:, cols]
        for t in range(tt):
            rows = slice(t * nb, (t + 1) * nb)
            hr_new = ar * hr - ai * hi + bure[rows, cols]
            hi_new = ar * hi + ai * hr + buim[rows, cols]
            hr, hi = hr_new, hi_new
            bure[rows, cols] = hr
            buim[rows, cols] = hi
        sre_ref[:, cols] = hr
        sim_ref[:, cols] = hi

    ys = []
    for k in range(S5_BLOCKS):
        cols = slice(k * S5_STATE_BLOCK, (k + 1) * S5_STATE_BLOCK)
        ys.append(_dot(bure[:, cols].astype(_BF16), wcr_ref[k])
                  - _dot(buim[:, cols].astype(_BF16), wci_ref[k]))
    y = jnp.concatenate(ys, axis=1) + d_ref[...] * u
    z = jax.nn.gelu(y).astype(_BF16)
    glu = _dot(z, wglu_ref[...])
    s5_out = glu[:, 0:D_MODEL] * _sigmoid_of_twice(glu[:, D_MODEL:])

    g_lru = _sigmoid_of_twice(_dot(xb, w_in_ref[:, D_MODEL + D_S5:2 * D_MODEL + D_S5]))
    g_s5 = _sigmoid_of_twice(_dot(xb, w_in_ref[:, 2 * D_MODEL + D_S5:]))
    merged = (g_lru * hbuf[...] + g_s5 * s5_out).astype(_BF16)
    mix = _dot(merged, wout_ref[...])
    y1_ref[...] = ALPHA * load_x() + mix


def _ffn_kernel(nb, tt, batch_major,
                y1_ref, g1_ref, b1_ref, wup_ref, cw_ref, cb_ref, wdown_ref, g2_ref, b2_ref, conv0_ref,
                out_ref, conv_out_ref,
                abuf, *maybe_yt):
    m = nb * tt
    hist = (FFN_CONV - 1) * nb

    @pl.when(pl.program_id(0) == 0)
    def _():
        abuf[0:hist, :] = conv0_ref[...]

    x = _layer_norm(y1_ref[...], g1_ref[...], b1_ref[...])
    xb = x.astype(_BF16)
    for r0 in range(0, m, HEAD_ROWS):
        abuf[hist + r0:hist + r0 + HEAD_ROWS, 0:D_MODEL] = _dot(xb[r0:r0 + HEAD_ROWS, :], wup_ref[:, 0:D_MODEL])
    abuf[hist:hist + m, D_MODEL:] = _dot(xb, wup_ref[:, D_MODEL:D_FF])
    f = None
    for c in range(D_FF // D_MODEL):
        cols = slice(c * D_MODEL, (c + 1) * D_MODEL)
        ac = cb_ref[:, cols] + abuf[0:m, cols] * cw_ref[0:1, cols]
        for k in range(1, FFN_CONV):
            ac = ac + abuf[k * nb:k * nb + m, cols] * cw_ref[k:k + 1, cols]
        gate = _dot(xb, wup_ref[:, D_FF + c * D_MODEL:D_FF + (c + 1) * D_MODEL])
        hmid = (jax.nn.gelu(ac) * gate).astype(_BF16)
        part = _dot(hmid, wdown_ref[cols, :])
        f = part if f is None else f + part
    new_hist = abuf[m:m + hist, :]
    abuf[0:hist, :] = new_hist
    conv_out_ref[...] = new_hist
    y = _layer_norm(ALPHA * x + f, g2_ref[...], b2_ref[...])
    if batch_major:
        (yt,) = maybe_yt
        for c in range(LANE_BLOCKS):
            yt[c] = y[:, c * LANES:(c + 1) * LANES]
        for b in range(nb):
            for c in range(LANE_BLOCKS):
                out_ref[b, :, c * LANES:(c + 1) * LANES] = yt[c, pl.ds(b, tt, stride=nb), :]
    else:
        out_ref[...] = y


def _resident(shape):
    zeros = (0,) * len(shape)
    return pl.BlockSpec(shape, lambda i: zeros, pipeline_mode=pl.Buffered(1))


def _run_mixer(x, nb, tt, weights, states, batch_major):
    m = nb * tt
    hist = (LRU_CONV - 1) * nb
    tile = pl.BlockSpec((m, D_MODEL), lambda i: (i, 0))
    if batch_major:
        n_rows = nb * x.shape[1]
        x_spec = pl.BlockSpec((nb, tt, D_MODEL), lambda i: (0, i, 0))
    else:
        n_rows = x.shape[0]
        x_spec = tile
    operands = (x,) + tuple(weights) + tuple(states)
    in_specs = [x_spec] + [_resident(a.shape) for a in operands[1:]]
    out_shape = (
        jax.ShapeDtypeStruct((n_rows, D_MODEL), _F32),
        jax.ShapeDtypeStruct((hist, D_MODEL), _F32),
        jax.ShapeDtypeStruct((nb, D_MODEL), _F32),
        jax.ShapeDtypeStruct((nb, D_STATE), _F32),
        jax.ShapeDtypeStruct((nb, D_STATE), _F32),
    )
    out_specs = [tile] + [pl.BlockSpec(s.shape, lambda i: (0, 0)) for s in out_shape[1:]]
    scratch_shapes = [
        pltpu.VMEM((hist + m, D_MODEL), _F32),
        pltpu.VMEM((m, D_MODEL), _F32),
        pltpu.VMEM((m, D_MODEL), _F32),
        pltpu.VMEM((m, D_STATE), _F32),
        pltpu.VMEM((m, D_STATE), _F32),
    ]
    if batch_major:
        scratch_shapes.append(pltpu.VMEM((LANE_BLOCKS, m, LANES), _F32))
    return pl.pallas_call(
        functools.partial(_mixer_kernel, nb, tt, batch_major),
        out_shape=out_shape,
        grid=(n_rows // m,),
        in_specs=in_specs,
        out_specs=out_specs,
        scratch_shapes=scratch_shapes,
        compiler_params=pltpu.CompilerParams(
            dimension_semantics=("arbitrary",), vmem_limit_bytes=VMEM_LIMIT_BYTES),
        name=f"mixer_nb{nb}",
    )(*operands)


def _run_ffn(x_rows, nb, tt, weights, conv0, batch_major):
    n_rows = x_rows.shape[0]
    m = nb * tt
    hist = (FFN_CONV - 1) * nb
    tile = pl.BlockSpec((m, D_MODEL), lambda i: (i, 0))
    operands = (x_rows,) + tuple(weights) + (conv0,)
    in_specs = [tile] + [_resident(a.shape) for a in operands[1:]]
    if batch_major:
        y_shape = jax.ShapeDtypeStruct((nb, n_rows // nb, D_MODEL), _F32)
        y_spec = pl.BlockSpec((nb, tt, D_MODEL), lambda i: (0, i, 0))
    else:
        y_shape = jax.ShapeDtypeStruct((n_rows, D_MODEL), _F32)
        y_spec = tile
    out_shape = (y_shape, jax.ShapeDtypeStruct((hist, D_FF), _F32))
    out_specs = [y_spec, pl.BlockSpec((hist, D_FF), lambda i: (0, 0))]
    scratch_shapes = [pltpu.VMEM((hist + m, D_FF), _F32)]
    if batch_major:
        scratch_shapes.append(pltpu.VMEM((LANE_BLOCKS, m, LANES), _F32))
    return pl.pallas_call(
        functools.partial(_ffn_kernel, nb, tt, batch_major),
        out_shape=out_shape,
        grid=(n_rows // m,),
        in_specs=in_specs,
        out_specs=out_specs,
        scratch_shapes=scratch_shapes,
        compiler_params=pltpu.CompilerParams(
            dimension_semantics=("arbitrary",), vmem_limit_bytes=VMEM_LIMIT_BYTES),
        name=f"ffn_nb{nb}",
    )(*operands)


def _block_diag(blocks, n_per_tile):
    n, r, c = blocks.shape
    tiles = n // n_per_tile
    eye = jnp.eye(n_per_tile, dtype=blocks.dtype)
    b = blocks.reshape(tiles, n_per_tile, r, c)
    return jnp.einsum("tgrc,gh->tgrhc", b, eye).reshape(tiles, n_per_tile * r, n_per_tile * c)


def _s5_discretise(a_re, a_im, log_dt, b_re, b_im):
    dt = jnp.exp(log_dt)[:, None]
    mag = jnp.exp(a_re * dt)
    ab_re = mag * jnp.cos(a_im * dt)
    ab_im = mag * jnp.sin(a_im * dt)
    nr = ab_re - 1.0
    ni = ab_im
    den = a_re * a_re + a_im * a_im
    coef_re = (nr * a_re + ni * a_im) / den
    coef_im = (ni * a_re - nr * a_im) / den
    bb_re = coef_re[..., None] * b_re - coef_im[..., None] * b_im
    bb_im = coef_re[..., None] * b_im + coef_im[..., None] * b_re
    return ab_re, ab_im, bb_re, bb_im


def _time_major(x):
    b, l, c = x.shape
    return jnp.transpose(x, (1, 0, 2)).reshape(l * b, c)


def _batch_major(rows, b):
    n, c = rows.shape
    return jnp.transpose(rows.reshape(n // b, b, c), (1, 0, 2))


def _layer(x, lru_conv0, lru_h0, s5_re0, s5_im0, ffn_conv0, tt, mixer_w, ffn_w):
    nb = x.shape[0]
    batch_major = tt % SUBLANES == 0
    states = (_time_major(lru_conv0), lru_h0,
              s5_re0.reshape(nb, D_STATE), s5_im0.reshape(nb, D_STATE))
    x1, conv_out, h_out, sre, sim = _run_mixer(
        x if batch_major else _time_major(x), nb, tt, mixer_w, states, batch_major)
    y, fconv_out = _run_ffn(x1, nb, tt, ffn_w, _time_major(ffn_conv0), batch_major)
    return (y if batch_major else _batch_major(y, nb), _batch_major(conv_out, nb), h_out,
            sre.reshape(nb, S5_GROUPS, S5_STATE), sim.reshape(nb, S5_GROUPS, S5_STATE),
            _batch_major(fconv_out, nb))


def kernel(x_prompt, x_sample, state_lru_conv, state_lru_h, state_s5_re, state_s5_im, state_ffn_conv, w_in, lru_conv_w, lru_conv_b, lru_wa, lru_ba, lru_wx, lru_bx, lru_lambda, s5_a_re, s5_a_im, s5_log_dt, s5_b_re, s5_b_im, s5_c_re, s5_c_im, s5_d, w_glu, w_out, ln1_g, ln1_b, w_up, ffn_conv_w, ffn_conv_b, w_down, ln2_g, ln2_b):
    n_p = x_prompt.shape[0]
    xp, xs = x_prompt, x_sample
    outs_p, outs_s = [], []
    for l in range(DEPTH):
        row = lambda v: v[l].reshape(1, -1)
        heads_per_tile = MXU_TILE // LRU_HEAD_DIM
        wg = (0.5 * jnp.concatenate([_block_diag(lru_wa[l], heads_per_tile),
                                     _block_diag(lru_wx[l], heads_per_tile)], axis=2)).astype(_BF16)
        w_in_scale = jnp.concatenate([jnp.ones((D_MODEL + D_S5,), _F32), jnp.full((2 * D_MODEL,), 0.5, _F32)])
        w_glu_scale = jnp.concatenate([jnp.ones((D_MODEL,), _F32), jnp.full((D_MODEL,), 0.5, _F32)])
        ab_re, ab_im, bb_re, bb_im = _s5_discretise(s5_a_re[l], s5_a_im[l], s5_log_dt[l], s5_b_re[l], s5_b_im[l])
        groups_per_block = S5_IN_BLOCK // S5_GROUP
        to_in = lambda bb: _block_diag(jnp.swapaxes(bb, 1, 2), groups_per_block)
        to_out = lambda cc: _block_diag(jnp.swapaxes(cc, 1, 2), groups_per_block)
        wb = jnp.concatenate([to_in(bb_re), to_in(bb_im)], axis=2).astype(_BF16)
        mixer_w = (
            (w_in[l] * w_in_scale).astype(_BF16), lru_conv_w[l], row(lru_conv_b), wg,
            0.5 * row(lru_ba), 0.5 * row(lru_bx), row(lru_lambda),
            wb, ab_re.reshape(1, D_STATE), ab_im.reshape(1, D_STATE),
            to_out(s5_c_re[l]).astype(_BF16), to_out(s5_c_im[l]).astype(_BF16), row(s5_d),
            (w_glu[l] * w_glu_scale).astype(_BF16), w_out[l].astype(_BF16),
        )
        ffn_w = (row(ln1_g), row(ln1_b), w_up[l].astype(_BF16), ffn_conv_w[l], row(ffn_conv_b), w_down[l].astype(_BF16),
                 row(ln2_g), row(ln2_b))

        zc = jnp.zeros((n_p, LRU_CONV - 1, D_MODEL), _F32)
        zh = jnp.zeros((n_p, D_MODEL), _F32)
        zs = jnp.zeros((n_p, S5_GROUPS, S5_STATE), _F32)
        zf = jnp.zeros((n_p, FFN_CONV - 1, D_FF), _F32)
        xp, *st_p = _layer(xp, zc, zh, zs, zs, zf, PROMPT_TILE_STEPS, mixer_w, ffn_w)
        outs_p.append(st_p)
        xs, *st_s = _layer(xs, state_lru_conv[l], state_lru_h[l], state_s5_re[l], state_s5_im[l],
                           state_ffn_conv[l], xs.shape[1], mixer_w, ffn_w)
        outs_s.append(st_s)
    stack = lambda outs, j: jnp.stack([o[j] for o in outs])
    return (xp, xs,
            *(stack(outs_p, j) for j in range(5)),
            *(stack(outs_s, j) for j in range(5)))
```

```python
import functools
import math

import jax
import jax.numpy as jnp
from jax.experimental import pallas as pl
from jax.experimental.pallas import tpu as pltpu

D_MODEL = 1024
LRU_HEADS = 16
LRU_HEAD_DIM = D_MODEL // LRU_HEADS
LRU_CONV = 4
LRU_C = 8.0
S5_GROUP = 16
D_S5 = D_MODEL // 2
S5_GROUPS = D_S5 // S5_GROUP
S5_STATE = 64
D_STATE = S5_GROUPS * S5_STATE
D_FF = 3 * D_MODEL
FFN_CONV = 3
DEPTH = 1
ALPHA = (2.0 * DEPTH) ** 0.25
LN_EPS = 1e-5
LOG2_E = math.log2(math.e)
RSQRT_FLOOR = 1e-36

MXU_TILE = 256
SUBLANES = 8
LANES = 128
LANE_BLOCKS = D_MODEL // LANES
GATE_BLOCKS = D_MODEL // MXU_TILE
S5_IN_BLOCK = 128
S5_BLOCKS = D_S5 // S5_IN_BLOCK
S5_STATE_BLOCK = D_STATE // S5_BLOCKS
PROMPT_TILE_STEPS = 64
HEAD_ROWS = 256
VMEM_LIMIT_BYTES = 56 * 1024 * 1024

_BF16 = jnp.bfloat16
_F32 = jnp.float32


def _dot(a, b):
    return jnp.dot(a, b, preferred_element_type=_F32)


def _sigmoid_of_twice(half_z):
    return 0.5 * jnp.tanh(half_z) + 0.5


def _layer_norm(y, g, b):
    mu = jnp.mean(y, axis=-1, keepdims=True)
    yc = y - mu
    var = jnp.mean(yc * yc, axis=-1, keepdims=True)
    return yc * jax.lax.rsqrt(var + LN_EPS) * g + b


def _mixer_kernel(nb, tt, batch_major,
                  x_ref, w_in_ref, cw_ref, cb_ref, wg_ref, ba_ref, bx_ref, lam_ref,
                  wb_ref, are_ref, aim_ref, wcr_ref, wci_ref, d_ref, wglu_ref, wout_ref,
                  conv0_ref, h0_ref, sre0_ref, sim0_ref,
                  y1_ref, conv_out_ref, h_ref, sre_ref, sim_ref,
                  xbuf, abuf, hbuf, bure, buim, gpre, *maybe_xt):
    m = nb * tt
    hist = (LRU_CONV - 1) * nb

    @pl.when(pl.program_id(0) == 0)
    def _():
        xbuf[0:hist, :] = conv0_ref[...]
        h_ref[...] = h0_ref[...]
        sre_ref[...] = sre0_ref[...]
        sim_ref[...] = sim0_ref[...]

    if batch_major:
        (xt,) = maybe_xt
        for t0 in range(0, tt, SUBLANES):
            for b in range(nb):
                for c in range(LANE_BLOCKS):
                    xt[c, pl.ds(t0 * nb + b, SUBLANES, stride=nb), :] = (
                        x_ref[b, t0:t0 + SUBLANES, c * LANES:(c + 1) * LANES])
        load_x = lambda: jnp.concatenate([xt[c] for c in range(LANE_BLOCKS)], axis=1)
    else:
        load_x = lambda: x_ref[...]
    xb = load_x().astype(_BF16)

    lam = lam_ref[...]
    c_lam = (0.5 * LRU_C * LOG2_E) * (jnp.minimum(lam, 0.0) - jnp.log1p(jnp.exp(-jnp.abs(lam))))
    gate_cols = D_MODEL + D_S5

    def lru_input_projection(k):
        blk = slice(k * MXU_TILE, (k + 1) * MXU_TILE)
        for r0 in range(0, m, HEAD_ROWS if k == 0 else m):
            rows = slice(r0, r0 + (HEAD_ROWS if k == 0 else m))
            xbuf[hist + rows.start:hist + rows.stop, blk] = _dot(xb[rows, :], w_in_ref[:, blk])

    lru_input_projection(0)
    for k in range(GATE_BLOCKS):
        blk = slice(k * MXU_TILE, (k + 1) * MXU_TILE)
        if k + 1 < GATE_BLOCKS:
            lru_input_projection(k + 1)
        if k == 0:
            u = _dot(xb, w_in_ref[:, D_MODEL:D_MODEL + D_S5])
            ub = u.astype(_BF16)
            for j in range(S5_BLOCKS):
                bu = _dot(ub[:, j * S5_IN_BLOCK:(j + 1) * S5_IN_BLOCK], wb_ref[j])
                cols = slice(j * S5_STATE_BLOCK, (j + 1) * S5_STATE_BLOCK)
                bure[:, cols] = bu[:, 0:S5_STATE_BLOCK]
                buim[:, cols] = bu[:, S5_STATE_BLOCK:]
        xc = cb_ref[:, blk] + xbuf[0:m, blk] * cw_ref[0:1, blk]
        for j in range(1, LRU_CONV):
            xc = xc + xbuf[j * nb:j * nb + m, blk] * cw_ref[j:j + 1, blk]
        new_hist = xbuf[m:m + hist, blk]
        xbuf[0:hist, blk] = new_hist
        conv_out_ref[:, blk] = new_hist
        half_g = _dot(xc.astype(_BF16), wg_ref[k])
        tanh_r = jnp.tanh(half_g[:, 0:MXU_TILE] + ba_ref[:, blk])
        ig = _sigmoid_of_twice(half_g[:, MXU_TILE:] + bx_ref[:, blk])
        a = jnp.exp2(tanh_r * c_lam[:, blk] + c_lam[:, blk])
        abuf[:, blk] = a
        one_minus_a2 = jnp.maximum((1.0 - a) * (1.0 + a), 0.0)
        mult = one_minus_a2 * jax.lax.rsqrt(jnp.maximum(one_minus_a2, RSQRT_FLOOR))
        hbuf[:, blk] = mult * (ig * xc)
        for half in range(2):
            gblk = slice((2 * k + half) * MXU_TILE, (2 * k + half + 1) * MXU_TILE)
            gpre[:, gblk] = _dot(xb, w_in_ref[:, gate_cols + gblk.start:gate_cols + gblk.stop])

    h = h_ref[...]
    for t in range(tt):
        rows = slice(t * nb, (t + 1) * nb)
        h = abuf[rows, :] * h + hbuf[rows, :]
        hbuf[rows, :] = h
    h_ref[...] = h

    ys = []
    for k in range(S5_BLOCKS):
        cols = slice(k * S5_STATE_BLOCK, (k + 1) * S5_STATE_BLOCK)
        ar = jnp.broadcast_to(are_ref[:, cols], (nb, S5_STATE_BLOCK))
        ai = jnp.broadcast_to(aim_ref[:, cols], (nb, S5_STATE_BLOCK))
        hr = sre_ref[:, cols]
        hi = sim_ref[:, cols]
        for t in range(tt):
            rows = slice(t * nb, (t + 1) * nb)
            hr_new = ar * hr - ai * hi + bure[rows, cols]
            hi_new = ar * hi + ai * hr + buim[rows, cols]
            hr, hi = hr_new, hi_new
            bure[rows, cols] = hr
            buim[rows, cols] = hi
        sre_ref[:, cols] = hr
        sim_ref[:, cols] = hi
        ys.append(_dot(bure[:, cols].astype(_BF16), wcr_ref[k])
                  - _dot(buim[:, cols].astype(_BF16), wci_ref[k]))
    y = jnp.concatenate(ys, axis=1) + d_ref[...] * u
    z = jax.nn.gelu(y).astype(_BF16)
    glu = _dot(z, wglu_ref[...])
    s5_out = glu[:, 0:D_MODEL] * _sigmoid_of_twice(glu[:, D_MODEL:])

    g_lru = _sigmoid_of_twice(gpre[:, 0:D_MODEL])
    g_s5 = _sigmoid_of_twice(gpre[:, D_MODEL:])
    merged = (g_lru * hbuf[...] + g_s5 * s5_out).astype(_BF16)
    mix = _dot(merged, wout_ref[...])
    y1_ref[...] = ALPHA * load_x() + mix


def _ffn_kernel(nb, tt, batch_major,
                y1_ref, g1_ref, b1_ref, wup_ref, cw_ref, cb_ref, wdown_ref, g2_ref, b2_ref, conv0_ref,
                out_ref, conv_out_ref,
                abuf, *maybe_yt):
    m = nb * tt
    hist = (FFN_CONV - 1) * nb

    @pl.when(pl.program_id(0) == 0)
    def _():
        abuf[0:hist, :] = conv0_ref[...]

    x = _layer_norm(y1_ref[...], g1_ref[...], b1_ref[...])
    xb = x.astype(_BF16)
    for r0 in range(0, m, HEAD_ROWS):
        abuf[hist + r0:hist + r0 + HEAD_ROWS, 0:D_MODEL] = _dot(xb[r0:r0 + HEAD_ROWS, :], wup_ref[:, 0:D_MODEL])
    abuf[hist:hist + m, D_MODEL:] = _dot(xb, wup_ref[:, D_MODEL:D_FF])
    f = None
    for c in range(D_FF // D_MODEL):
        cols = slice(c * D_MODEL, (c + 1) * D_MODEL)
        ac = cb_ref[:, cols] + abuf[0:m, cols] * cw_ref[0:1, cols]
        for k in range(1, FFN_CONV):
            ac = ac + abuf[k * nb:k * nb + m, cols] * cw_ref[k:k + 1, cols]
        gate = _dot(xb, wup_ref[:, D_FF + c * D_MODEL:D_FF + (c + 1) * D_MODEL])
        hmid = (jax.nn.gelu(ac) * gate).astype(_BF16)
        part = _dot(hmid, wdown_ref[cols, :])
        f = part if f is None else f + part
    new_hist = abuf[m:m + hist, :]
    abuf[0:hist, :] = new_hist
    conv_out_ref[...] = new_hist
    y = _layer_norm(ALPHA * x + f, g2_ref[...], b2_ref[...])
    if batch_major:
        (yt,) = maybe_yt
        for c in range(LANE_BLOCKS):
            yt[c] = y[:, c * LANES:(c + 1) * LANES]
        for b in range(nb):
            for c in range(LANE_BLOCKS):
                out_ref[b, :, c * LANES:(c + 1) * LANES] = yt[c, pl.ds(b, tt, stride=nb), :]
    else:
        out_ref[...] = y


def _resident(shape):
    zeros = (0,) * len(shape)
    return pl.BlockSpec(shape, lambda i: zeros, pipeline_mode=pl.Buffered(1))


def _run_mixer(x, nb, tt, weights, states, batch_major):
    m = nb * tt
    hist = (LRU_CONV - 1) * nb
    tile = pl.BlockSpec((m, D_MODEL), lambda i: (i, 0))
    if batch_major:
        n_rows = nb * x.shape[1]
        x_spec = pl.BlockSpec((nb, tt, D_MODEL), lambda i: (0, i, 0))
    else:
        n_rows = x.shape[0]
        x_spec = tile
    operands = (x,) + tuple(weights) + tuple(states)
    in_specs = [x_spec] + [_resident(a.shape) for a in operands[1:]]
    out_shape = (
        jax.ShapeDtypeStruct((n_rows, D_MODEL), _F32),
        jax.ShapeDtypeStruct((hist, D_MODEL), _F32),
        jax.ShapeDtypeStruct((nb, D_MODEL), _F32),
        jax.ShapeDtypeStruct((nb, D_STATE), _F32),
        jax.ShapeDtypeStruct((nb, D_STATE), _F32),
    )
    out_specs = [tile] + [pl.BlockSpec(s.shape, lambda i: (0, 0)) for s in out_shape[1:]]
    scratch_shapes = [
        pltpu.VMEM((hist + m, D_MODEL), _F32),
        pltpu.VMEM((m, D_MODEL), _F32),
        pltpu.VMEM((m, D_MODEL), _F32),
        pltpu.VMEM((m, D_STATE), _F32),
        pltpu.VMEM((m, D_STATE), _F32),
        pltpu.VMEM((m, 2 * D_MODEL), _F32),
    ]
    if batch_major:
        scratch_shapes.append(pltpu.VMEM((LANE_BLOCKS, m, LANES), _F32))
    return pl.pallas_call(
        functools.partial(_mixer_kernel, nb, tt, batch_major),
        out_shape=out_shape,
        grid=(n_rows // m,),
        in_specs=in_specs,
        out_specs=out_specs,
        scratch_shapes=scratch_shapes,
        compiler_params=pltpu.CompilerParams(
            dimension_semantics=("arbitrary",), vmem_limit_bytes=VMEM_LIMIT_BYTES),
        name=f"mixer_nb{nb}",
    )(*operands)


def _run_ffn(x_rows, nb, tt, weights, conv0, batch_major):
    n_rows = x_rows.shape[0]
    m = nb * tt
    hist = (FFN_CONV - 1) * nb
    tile = pl.BlockSpec((m, D_MODEL), lambda i: (i, 0))
    operands = (x_rows,) + tuple(weights) + (conv0,)
    in_specs = [tile] + [_resident(a.shape) for a in operands[1:]]
    if batch_major:
        y_shape = jax.ShapeDtypeStruct((nb, n_rows // nb, D_MODEL), _F32)
        y_spec = pl.BlockSpec((nb, tt, D_MODEL), lambda i: (0, i, 0))
    else:
        y_shape = jax.ShapeDtypeStruct((n_rows, D_MODEL), _F32)
        y_spec = tile
    out_shape = (y_shape, jax.ShapeDtypeStruct((hist, D_FF), _F32))
    out_specs = [y_spec, pl.BlockSpec((hist, D_FF), lambda i: (0, 0))]
    scratch_shapes = [pltpu.VMEM((hist + m, D_FF), _F32)]
    if batch_major:
        scratch_shapes.append(pltpu.VMEM((LANE_BLOCKS, m, LANES), _F32))
    return pl.pallas_call(
        functools.partial(_ffn_kernel, nb, tt, batch_major),
        out_shape=out_shape,
        grid=(n_rows // m,),
        in_specs=in_specs,
        out_specs=out_specs,
        scratch_shapes=scratch_shapes,
        compiler_params=pltpu.CompilerParams(
            dimension_semantics=("arbitrary",), vmem_limit_bytes=VMEM_LIMIT_BYTES),
        name=f"ffn_nb{nb}",
    )(*operands)


def _block_diag(blocks, n_per_tile):
    n, r, c = blocks.shape
    tiles = n // n_per_tile
    eye = jnp.eye(n_per_tile, dtype=blocks.dtype)
    b = blocks.reshape(tiles, n_per_tile, r, c)
    return jnp.einsum("tgrc,gh->tgrhc", b, eye).reshape(tiles, n_per_tile * r, n_per_tile * c)


def _s5_discretise(a_re, a_im, log_dt, b_re, b_im):
    dt = jnp.exp(log_dt)[:, None]
    mag = jnp.exp(a_re * dt)
    ab_re = mag * jnp.cos(a_im * dt)
    ab_im = mag * jnp.sin(a_im * dt)
    nr = ab_re - 1.0
    ni = ab_im
    den = a_re * a_re + a_im * a_im
    coef_re = (nr * a_re + ni * a_im) / den
    coef_im = (ni * a_re - nr * a_im) / den
    bb_re = coef_re[..., None] * b_re - coef_im[..., None] * b_im
    bb_im = coef_re[..., None] * b_im + coef_im[..., None] * b_re
    return ab_re, ab_im, bb_re, bb_im


def _time_major(x):
    b, l, c = x.shape
    return jnp.transpose(x, (1, 0, 2)).reshape(l * b, c)


def _batch_major(rows, b):
    n, c = rows.shape
    return jnp.transpose(rows.reshape(n // b, b, c), (1, 0, 2))


def _layer(x, lru_conv0, lru_h0, s5_re0, s5_im0, ffn_conv0, tt, mixer_w, ffn_w):
    nb = x.shape[0]
    batch_major = tt % SUBLANES == 0
    states = (_time_major(lru_conv0), lru_h0,
              s5_re0.reshape(nb, D_STATE), s5_im0.reshape(nb, D_STATE))
    x1, conv_out, h_out, sre, sim = _run_mixer(
        x if batch_major else _time_major(x), nb, tt, mixer_w, states, batch_major)
    y, fconv_out = _run_ffn(x1, nb, tt, ffn_w, _time_major(ffn_conv0), batch_major)
    return (y if batch_major else _batch_major(y, nb), _batch_major(conv_out, nb), h_out,
            sre.reshape(nb, S5_GROUPS, S5_STATE), sim.reshape(nb, S5_GROUPS, S5_STATE),
            _batch_major(fconv_out, nb))


def kernel(x_prompt, x_sample, state_lru_conv, state_lru_h, state_s5_re, state_s5_im, state_ffn_conv, w_in, lru_conv_w, lru_conv_b, lru_wa, lru_ba, lru_wx, lru_bx, lru_lambda, s5_a_re, s5_a_im, s5_log_dt, s5_b_re, s5_b_im, s5_c_re, s5_c_im, s5_d, w_glu, w_out, ln1_g, ln1_b, w_up, ffn_conv_w, ffn_conv_b, w_down, ln2_g, ln2_b):
    n_p = x_prompt.shape[0]
    xp, xs = x_prompt, x_sample
    outs_p, outs_s = [], []
    for l in range(DEPTH):
        row = lambda v: v[l].reshape(1, -1)
        heads_per_tile = MXU_TILE // LRU_HEAD_DIM
        wg = (0.5 * jnp.concatenate([_block_diag(lru_wa[l], heads_per_tile),
                                     _block_diag(lru_wx[l], heads_per_tile)], axis=2)).astype(_BF16)
        w_in_scale = jnp.concatenate([jnp.ones((D_MODEL + D_S5,), _F32), jnp.full((2 * D_MODEL,), 0.5, _F32)])
        w_glu_scale = jnp.concatenate([jnp.ones((D_MODEL,), _F32), jnp.full((D_MODEL,), 0.5, _F32)])
        ab_re, ab_im, bb_re, bb_im = _s5_discretise(s5_a_re[l], s5_a_im[l], s5_log_dt[l], s5_b_re[l], s5_b_im[l])
        groups_per_block = S5_IN_BLOCK // S5_GROUP
        to_in = lambda bb: _block_diag(jnp.swapaxes(bb, 1, 2), groups_per_block)
        to_out = lambda cc: _block_diag(jnp.swapaxes(cc, 1, 2), groups_per_block)
        wb = jnp.concatenate([to_in(bb_re), to_in(bb_im)], axis=2).astype(_BF16)
        mixer_w = (
            (w_in[l] * w_in_scale).astype(_BF16), lru_conv_w[l], row(lru_conv_b), wg,
            0.5 * row(lru_ba), 0.5 * row(lru_bx), row(lru_lambda),
            wb, ab_re.reshape(1, D_STATE), ab_im.reshape(1, D_STATE),
            to_out(s5_c_re[l]).astype(_BF16), to_out(s5_c_im[l]).astype(_BF16), row(s5_d),
            (w_glu[l] * w_glu_scale).astype(_BF16), w_out[l].astype(_BF16),
        )
        ffn_w = (row(ln1_g), row(ln1_b), w_up[l].astype(_BF16), ffn_conv_w[l], row(ffn_conv_b), w_down[l].astype(_BF16),
                 row(ln2_g), row(ln2_b))

        zc = jnp.zeros((n_p, LRU_CONV - 1, D_MODEL), _F32)
        zh = jnp.zeros((n_p, D_MODEL), _F32)
        zs = jnp.zeros((n_p, S5_GROUPS, S5_STATE), _F32)
        zf = jnp.zeros((n_p, FFN_CONV - 1, D_FF), _F32)
        xp, *st_p = _layer(xp, zc, zh, zs, zs, zf, PROMPT_TILE_STEPS, mixer_w, ffn_w)
        outs_p.append(st_p)
        xs, *st_s = _layer(xs, state_lru_conv[l], state_lru_h[l], state_s5_re[l], state_s5_im[l],
                           state_ffn_conv[l], xs.shape[1], mixer_w, ffn_w)
        outs_s.append(st_s)
    stack = lambda outs, j: jnp.stack([o[j] for o in outs])
    return (xp, xs,
            *(stack(outs_p, j) for j in range(5)),
            *(stack(outs_s, j) for j in range(5)))
```

```python
import functools
import math

import jax
import jax.numpy as jnp
from jax.experimental import pallas as pl
from jax.experimental.pallas import tpu as pltpu

D_MODEL = 1024
LRU_HEADS = 16
LRU_HEAD_DIM = D_MODEL // LRU_HEADS
LRU_CONV = 4
LRU_C = 8.0
S5_GROUP = 16
D_S5 = D_MODEL // 2
S5_GROUPS = D_S5 // S5_GROUP
S5_STATE = 64
D_STATE = S5_GROUPS * S5_STATE
D_FF = 3 * D_MODEL
FFN_CONV = 3
DEPTH = 1
ALPHA = (2.0 * DEPTH) ** 0.25
LN_EPS = 1e-5
LOG2_E = math.log2(math.e)
RSQRT_FLOOR = 1e-36

MXU_TILE = 256
SUBLANES = 8
LANES = 128
LANE_BLOCKS = D_MODEL // LANES
GATE_BLOCKS = D_MODEL // MXU_TILE
S5_IN_BLOCK = 128
S5_BLOCKS = D_S5 // S5_IN_BLOCK
S5_STATE_BLOCK = D_STATE // S5_BLOCKS
PROMPT_TILE_STEPS = 64
HEAD_ROWS = 256
VMEM_LIMIT_BYTES = 56 * 1024 * 1024

_BF16 = jnp.bfloat16
_F32 = jnp.float32


def _dot(a, b):
    return jnp.dot(a, b, preferred_element_type=_F32)


def _sigmoid_of_twice(half_z):
    return 0.5 * jnp.tanh(half_z) + 0.5


def _layer_norm(y, g, b):
    mu = jnp.mean(y, axis=-1, keepdims=True)
    yc = y - mu
    var = jnp.mean(yc * yc, axis=-1, keepdims=True)
    return yc * jax.lax.rsqrt(var + LN_EPS) * g + b


def _mixer_kernel(nb, tt, batch_major,
                  x_ref, w_in_ref, cw_ref, cb_ref, wg_ref, ba_ref, bx_ref, lam_ref,
                  wb_ref, are_ref, aim_ref, wcr_ref, wci_ref, d_ref, wglu_ref, wout_ref,
                  conv0_ref, h0_ref, sre0_ref, sim0_ref,
                  y1_ref, conv_out_ref, h_ref, sre_ref, sim_ref,
                  xbuf, abuf, hbuf, bure, buim, gpre, *maybe_xt):
    m = nb * tt
    hist = (LRU_CONV - 1) * nb

    @pl.when(pl.program_id(0) == 0)
    def _():
        xbuf[0:hist, :] = conv0_ref[...]
        h_ref[...] = h0_ref[...]
        sre_ref[...] = sre0_ref[...]
        sim_ref[...] = sim0_ref[...]

    if batch_major:
        (xt,) = maybe_xt
        for t0 in range(0, tt, SUBLANES):
            for b in range(nb):
                for c in range(LANE_BLOCKS):
                    xt[c, pl.ds(t0 * nb + b, SUBLANES, stride=nb), :] = (
                        x_ref[b, t0:t0 + SUBLANES, c * LANES:(c + 1) * LANES])
        load_x = lambda: jnp.concatenate([xt[c] for c in range(LANE_BLOCKS)], axis=1)
    else:
        load_x = lambda: x_ref[...]
    xb = load_x().astype(_BF16)

    lam = lam_ref[...]
    c_lam = (0.5 * LRU_C * LOG2_E) * (jnp.minimum(lam, 0.0) - jnp.log1p(jnp.exp(-jnp.abs(lam))))
    gate_cols = D_MODEL + D_S5

    def lru_input_projection(k):
        blk = slice(k * MXU_TILE, (k + 1) * MXU_TILE)
        for r0 in range(0, m, HEAD_ROWS if k == 0 else m):
            rows = slice(r0, r0 + (HEAD_ROWS if k == 0 else m))
            xbuf[hist + rows.start:hist + rows.stop, blk] = _dot(xb[rows, :], w_in_ref[:, blk])

    lru_input_projection(0)
    for k in range(GATE_BLOCKS):
        blk = slice(k * MXU_TILE, (k + 1) * MXU_TILE)
        if k + 1 < GATE_BLOCKS:
            lru_input_projection(k + 1)
        if k == 0:
            u = _dot(xb, w_in_ref[:, D_MODEL:D_MODEL + D_S5])
            ub = u.astype(_BF16)
            for j in range(S5_BLOCKS):
                bu = _dot(ub[:, j * S5_IN_BLOCK:(j + 1) * S5_IN_BLOCK], wb_ref[j])
                cols = slice(j * S5_STATE_BLOCK, (j + 1) * S5_STATE_BLOCK)
                bure[:, cols] = bu[:, 0:S5_STATE_BLOCK]
                buim[:, cols] = bu[:, S5_STATE_BLOCK:]
        xc = cb_ref[:, blk] + xbuf[0:m, blk] * cw_ref[0:1, blk]
        for j in range(1, LRU_CONV):
            xc = xc + xbuf[j * nb:j * nb + m, blk] * cw_ref[j:j + 1, blk]
        new_hist = xbuf[m:m + hist, blk]
        xbuf[0:hist, blk] = new_hist
        conv_out_ref[:, blk] = new_hist
        half_g = _dot(xc.astype(_BF16), wg_ref[k])
        tanh_r = jnp.tanh(half_g[:, 0:MXU_TILE] + ba_ref[:, blk])
        ig = _sigmoid_of_twice(half_g[:, MXU_TILE:] + bx_ref[:, blk])
        a = jnp.exp2(tanh_r * c_lam[:, blk] + c_lam[:, blk])
        abuf[:, blk] = a
        one_minus_a2 = jnp.maximum((1.0 - a) * (1.0 + a), 0.0)
        mult = one_minus_a2 * jax.lax.rsqrt(jnp.maximum(one_minus_a2, RSQRT_FLOOR))
        hbuf[:, blk] = mult * (ig * xc)
        for half in range(2):
            gblk = slice((2 * k + half) * MXU_TILE, (2 * k + half + 1) * MXU_TILE)
            gpre[:, gblk] = _dot(xb, w_in_ref[:, gate_cols + gblk.start:gate_cols + gblk.stop])

    h = h_ref[...]
    for t in range(tt):
        rows = slice(t * nb, (t + 1) * nb)
        h = abuf[rows, :] * h + hbuf[rows, :]
        hbuf[rows, :] = h
    h_ref[...] = h

    ys = []
    for k in range(S5_BLOCKS):
        cols = slice(k * S5_STATE_BLOCK, (k + 1) * S5_STATE_BLOCK)
        ar = jnp.broadcast_to(are_ref[:, cols], (nb, S5_STATE_BLOCK))
        ai = jnp.broadcast_to(aim_ref[:, cols], (nb, S5_STATE_BLOCK))
        hr = sre_ref[:, cols]
        hi = sim_ref[:, cols]
        for t in range(tt):
            rows = slice(t * nb, (t + 1) * nb)
            hr_new = ar * hr - ai * hi + bure[rows, cols]
            hi_new = ar * hi + ai * hr + buim[rows, cols]
            hr, hi = hr_new, hi_new
            bure[rows, cols] = hr
            buim[rows, cols] = hi
        sre_ref[:, cols] = hr
        sim_ref[:, cols] = hi
        ys.append(_dot(bure[:, cols].astype(_BF16), wcr_ref[k])
                  - _dot(buim[:, cols].astype(_BF16), wci_ref[k]))
    y = jnp.concatenate(ys, axis=1) + d_ref[...] * u
    z = jax.nn.gelu(y).astype(_BF16)
    glu = _dot(z, wglu_ref[...])
    s5_out = glu[:, 0:D_MODEL] * _sigmoid_of_twice(glu[:, D_MODEL:])

    g_lru = _sigmoid_of_twice(gpre[:, 0:D_MODEL])
    g_s5 = _sigmoid_of_twice(gpre[:, D_MODEL:])
    merged = (g_lru * hbuf[...] + g_s5 * s5_out).astype(_BF16)
    mix = _dot(merged, wout_ref[...])
    y1_ref[...] = ALPHA * load_x() + mix


def _ffn_kernel(nb, tt, n_tiles, batch_major,
                y1_ref, g1_ref, b1_ref, wup_ref, cw_ref, cb_ref, wdown_ref, g2_ref, b2_ref, conv0_ref,
                out_ref, conv_out_ref,
                abuf, *maybe_yt):
    m = nb * tt
    hist = (FFN_CONV - 1) * nb
    step = pl.program_id(0)

    @pl.when(step == 0)
    def _():
        abuf[0:hist, :] = conv0_ref[...]

    if batch_major:
        yt, y_sems = maybe_yt
        slot = step % 2

        def tile_copies(tile, staged):
            return [pltpu.make_async_copy(yt.at[staged, :, b, :],
                                          out_ref.at[b, pl.ds(tile * tt, tt), :],
                                          y_sems.at[staged, b]) for b in range(nb)]

        @pl.when(step >= 2)
        def _():
            for copy in tile_copies(step - 2, slot):
                copy.wait()

    x = _layer_norm(y1_ref[...], g1_ref[...], b1_ref[...])
    xb = x.astype(_BF16)
    for r0 in range(0, m, HEAD_ROWS):
        abuf[hist + r0:hist + r0 + HEAD_ROWS, 0:D_MODEL] = _dot(xb[r0:r0 + HEAD_ROWS, :], wup_ref[:, 0:D_MODEL])
    abuf[hist:hist + m, D_MODEL:] = _dot(xb, wup_ref[:, D_MODEL:D_FF])
    f = None
    for c in range(D_FF // D_MODEL):
        cols = slice(c * D_MODEL, (c + 1) * D_MODEL)
        ac = cb_ref[:, cols] + abuf[0:m, cols] * cw_ref[0:1, cols]
        for k in range(1, FFN_CONV):
            ac = ac + abuf[k * nb:k * nb + m, cols] * cw_ref[k:k + 1, cols]
        gate = _dot(xb, wup_ref[:, D_FF + c * D_MODEL:D_FF + (c + 1) * D_MODEL])
        hmid = (jax.nn.gelu(ac) * gate).astype(_BF16)
        part = _dot(hmid, wdown_ref[cols, :])
        f = part if f is None else f + part
    new_hist = abuf[m:m + hist, :]
    abuf[0:hist, :] = new_hist
    conv_out_ref[...] = new_hist
    y = _layer_norm(ALPHA * x + f, g2_ref[...], b2_ref[...])
    if batch_major:
        yt[slot] = y.reshape(tt, nb, D_MODEL)
        for copy in tile_copies(step, slot):
            copy.start()

        @pl.when(step == n_tiles - 1)
        def _():
            if n_tiles > 1:
                for copy in tile_copies(step - 1, 1 - slot):
                    copy.wait()
            for copy in tile_copies(step, slot):
                copy.wait()
    else:
        out_ref[...] = y


def _resident(shape):
    zeros = (0,) * len(shape)
    return pl.BlockSpec(shape, lambda i: zeros, pipeline_mode=pl.Buffered(1))


def _run_mixer(x, nb, tt, weights, states, batch_major):
    m = nb * tt
    hist = (LRU_CONV - 1) * nb
    tile = pl.BlockSpec((m, D_MODEL), lambda i: (i, 0))
    if batch_major:
        n_rows = nb * x.shape[1]
        x_spec = pl.BlockSpec((nb, tt, D_MODEL), lambda i: (0, i, 0))
    else:
        n_rows = x.shape[0]
        x_spec = tile
    operands = (x,) + tuple(weights) + tuple(states)
    in_specs = [x_spec] + [_resident(a.shape) for a in operands[1:]]
    out_shape = (
        jax.ShapeDtypeStruct((n_rows, D_MODEL), _F32),
        jax.ShapeDtypeStruct((hist, D_MODEL), _F32),
        jax.ShapeDtypeStruct((nb, D_MODEL), _F32),
        jax.ShapeDtypeStruct((nb, D_STATE), _F32),
        jax.ShapeDtypeStruct((nb, D_STATE), _F32),
    )
    out_specs = [tile] + [pl.BlockSpec(s.shape, lambda i: (0, 0)) for s in out_shape[1:]]
    scratch_shapes = [
        pltpu.VMEM((hist + m, D_MODEL), _F32),
        pltpu.VMEM((m, D_MODEL), _F32),
        pltpu.VMEM((m, D_MODEL), _F32),
        pltpu.VMEM((m, D_STATE), _F32),
        pltpu.VMEM((m, D_STATE), _F32),
        pltpu.VMEM((m, 2 * D_MODEL), _F32),
    ]
    if batch_major:
        scratch_shapes.append(pltpu.VMEM((LANE_BLOCKS, m, LANES), _F32))
    return pl.pallas_call(
        functools.partial(_mixer_kernel, nb, tt, batch_major),
        out_shape=out_shape,
        grid=(n_rows // m,),
        in_specs=in_specs,
        out_specs=out_specs,
        scratch_shapes=scratch_shapes,
        compiler_params=pltpu.CompilerParams(
            dimension_semantics=("arbitrary",), vmem_limit_bytes=VMEM_LIMIT_BYTES),
        name=f"mixer_nb{nb}",
    )(*operands)


def _run_ffn(x_rows, nb, tt, weights, conv0, batch_major):
    n_rows = x_rows.shape[0]
    m = nb * tt
    hist = (FFN_CONV - 1) * nb
    tile = pl.BlockSpec((m, D_MODEL), lambda i: (i, 0))
    operands = (x_rows,) + tuple(weights) + (conv0,)
    in_specs = [tile] + [_resident(a.shape) for a in operands[1:]]
    if batch_major:
        y_shape = jax.ShapeDtypeStruct((nb, n_rows // nb, D_MODEL), _F32)
        y_spec = pl.BlockSpec(memory_space=pl.ANY)
    else:
        y_shape = jax.ShapeDtypeStruct((n_rows, D_MODEL), _F32)
        y_spec = tile
    out_shape = (y_shape, jax.ShapeDtypeStruct((hist, D_FF), _F32))
    out_specs = [y_spec, pl.BlockSpec((hist, D_FF), lambda i: (0, 0))]
    scratch_shapes = [pltpu.VMEM((hist + m, D_FF), _F32)]
    if batch_major:
        scratch_shapes += [pltpu.VMEM((2, tt, nb, D_MODEL), _F32), pltpu.SemaphoreType.DMA((2, nb))]
    return pl.pallas_call(
        functools.partial(_ffn_kernel, nb, tt, n_rows // m, batch_major),
        out_shape=out_shape,
        grid=(n_rows // m,),
        in_specs=in_specs,
        out_specs=out_specs,
        scratch_shapes=scratch_shapes,
        compiler_params=pltpu.CompilerParams(
            dimension_semantics=("arbitrary",), vmem_limit_bytes=VMEM_LIMIT_BYTES),
        name=f"ffn_nb{nb}",
    )(*operands)


def _block_diag(blocks, n_per_tile):
    n, r, c = blocks.shape
    tiles = n // n_per_tile
    eye = jnp.eye(n_per_tile, dtype=blocks.dtype)
    b = blocks.reshape(tiles, n_per_tile, r, c)
    return jnp.einsum("tgrc,gh->tgrhc", b, eye).reshape(tiles, n_per_tile * r, n_per_tile * c)


def _s5_discretise(a_re, a_im, log_dt, b_re, b_im):
    dt = jnp.exp(log_dt)[:, None]
    mag = jnp.exp(a_re * dt)
    ab_re = mag * jnp.cos(a_im * dt)
    ab_im = mag * jnp.sin(a_im * dt)
    nr = ab_re - 1.0
    ni = ab_im
    den = a_re * a_re + a_im * a_im
    coef_re = (nr * a_re + ni * a_im) / den
    coef_im = (ni * a_re - nr * a_im) / den
    bb_re = coef_re[..., None] * b_re - coef_im[..., None] * b_im
    bb_im = coef_re[..., None] * b_im + coef_im[..., None] * b_re
    return ab_re, ab_im, bb_re, bb_im


def _time_major(x):
    b, l, c = x.shape
    return jnp.transpose(x, (1, 0, 2)).reshape(l * b, c)


def _batch_major(rows, b):
    n, c = rows.shape
    return jnp.transpose(rows.reshape(n // b, b, c), (1, 0, 2))


def _layer(x, lru_conv0, lru_h0, s5_re0, s5_im0, ffn_conv0, tt, mixer_w, ffn_w):
    nb = x.shape[0]
    batch_major = tt % SUBLANES == 0
    states = (_time_major(lru_conv0), lru_h0,
              s5_re0.reshape(nb, D_STATE), s5_im0.reshape(nb, D_STATE))
    x1, conv_out, h_out, sre, sim = _run_mixer(
        x if batch_major else _time_major(x), nb, tt, mixer_w, states, batch_major)
    y, fconv_out = _run_ffn(x1, nb, tt, ffn_w, _time_major(ffn_conv0), batch_major)
    return (y if batch_major else _batch_major(y, nb), _batch_major(conv_out, nb), h_out,
            sre.reshape(nb, S5_GROUPS, S5_STATE), sim.reshape(nb, S5_GROUPS, S5_STATE),
            _batch_major(fconv_out, nb))


def kernel(x_prompt, x_sample, state_lru_conv, state_lru_h, state_s5_re, state_s5_im, state_ffn_conv, w_in, lru_conv_w, lru_conv_b, lru_wa, lru_ba, lru_wx, lru_bx, lru_lambda, s5_a_re, s5_a_im, s5_log_dt, s5_b_re, s5_b_im, s5_c_re, s5_c_im, s5_d, w_glu, w_out, ln1_g, ln1_b, w_up, ffn_conv_w, ffn_conv_b, w_down, ln2_g, ln2_b):
    n_p = x_prompt.shape[0]
    xp, xs = x_prompt, x_sample
    outs_p, outs_s = [], []
    for l in range(DEPTH):
        row = lambda v: v[l].reshape(1, -1)
        heads_per_tile = MXU_TILE // LRU_HEAD_DIM
        wg = (0.5 * jnp.concatenate([_block_diag(lru_wa[l], heads_per_tile),
                                     _block_diag(lru_wx[l], heads_per_tile)], axis=2)).astype(_BF16)
        w_in_scale = jnp.concatenate([jnp.ones((D_MODEL + D_S5,), _F32), jnp.full((2 * D_MODEL,), 0.5, _F32)])
        w_glu_scale = jnp.concatenate([jnp.ones((D_MODEL,), _F32), jnp.full((D_MODEL,), 0.5, _F32)])
        ab_re, ab_im, bb_re, bb_im = _s5_discretise(s5_a_re[l], s5_a_im[l], s5_log_dt[l], s5_b_re[l], s5_b_im[l])
        groups_per_block = S5_IN_BLOCK // S5_GROUP
        to_in = lambda bb: _block_diag(jnp.swapaxes(bb, 1, 2), groups_per_block)
        to_out = lambda cc: _block_diag(jnp.swapaxes(cc, 1, 2), groups_per_block)
        wb = jnp.concatenate([to_in(bb_re), to_in(bb_im)], axis=2).astype(_BF16)
        mixer_w = (
            (w_in[l] * w_in_scale).astype(_BF16), lru_conv_w[l], row(lru_conv_b), wg,
            0.5 * row(lru_ba), 0.5 * row(lru_bx), row(lru_lambda),
            wb, ab_re.reshape(1, D_STATE), ab_im.reshape(1, D_STATE),
            to_out(s5_c_re[l]).astype(_BF16), to_out(s5_c_im[l]).astype(_BF16), row(s5_d),
            (w_glu[l] * w_glu_scale).astype(_BF16), w_out[l].astype(_BF16),
        )
        ffn_w = (row(ln1_g), row(ln1_b), w_up[l].astype(_BF16), ffn_conv_w[l], row(ffn_conv_b), w_down[l].astype(_BF16),
                 row(ln2_g), row(ln2_b))

        zc = jnp.zeros((n_p, LRU_CONV - 1, D_MODEL), _F32)
        zh = jnp.zeros((n_p, D_MODEL), _F32)
        zs = jnp.zeros((n_p, S5_GROUPS, S5_STATE), _F32)
        zf = jnp.zeros((n_p, FFN_CONV - 1, D_FF), _F32)
        xp, *st_p = _layer(xp, zc, zh, zs, zs, zf, PROMPT_TILE_STEPS, mixer_w, ffn_w)
        outs_p.append(st_p)
        xs, *st_s = _layer(xs, state_lru_conv[l], state_lru_h[l], state_s5_re[l], state_s5_im[l],
                           state_ffn_conv[l], xs.shape[1], mixer_w, ffn_w)
        outs_s.append(st_s)
    stack = lambda outs, j: jnp.stack([o[j] for o in outs])
    return (xp, xs,
            *(stack(outs_p, j) for j in range(5)),
            *(stack(outs_s, j) for j in range(5)))
```

```python
import functools
import math

import jax
import jax.numpy as jnp
from jax.experimental import pallas as pl
from jax.experimental.pallas import tpu as pltpu

D_MODEL = 1024
LRU_HEADS = 16
LRU_HEAD_DIM = D_MODEL // LRU_HEADS
LRU_CONV = 4
LRU_C = 8.0
S5_GROUP = 16
D_S5 = D_MODEL // 2
S5_GROUPS = D_S5 // S5_GROUP
S5_STATE = 64
D_STATE = S5_GROUPS * S5_STATE
D_FF = 3 * D_MODEL
FFN_CONV = 3
DEPTH = 1
ALPHA = (2.0 * DEPTH) ** 0.25
LN_EPS = 1e-5
LOG2_E = math.log2(math.e)
RSQRT_FLOOR = 1e-36

MXU_TILE = 256
SUBLANES = 8
LANES = 128
LANE_BLOCKS = D_MODEL // LANES
GATE_BLOCKS = D_MODEL // MXU_TILE
S5_IN_BLOCK = 128
S5_BLOCKS = D_S5 // S5_IN_BLOCK
S5_STATE_BLOCK = D_STATE // S5_BLOCKS
PROMPT_TILE_STEPS = 64
HEAD_ROWS = 256
VMEM_LIMIT_BYTES = 56 * 1024 * 1024

_BF16 = jnp.bfloat16
_F32 = jnp.float32


def _dot(a, b):
    return jnp.dot(a, b, preferred_element_type=_F32)


def _sigmoid_of_twice(half_z):
    return 0.5 * jnp.tanh(half_z) + 0.5


def _layer_norm(y, g, b):
    mu = jnp.mean(y, axis=-1, keepdims=True)
    yc = y - mu
    var = jnp.mean(yc * yc, axis=-1, keepdims=True)
    return yc * jax.lax.rsqrt(var + LN_EPS) * g + b


def _mixer_kernel(nb, tt, n_tiles, batch_major,
                  x_ref, w_in_ref, cw_ref, cb_ref, wg_ref, ba_ref, bx_ref, lam_ref,
                  wb_ref, are_ref, aim_ref, wcr_ref, wci_ref, d_ref, wglu_ref, wout_ref,
                  conv0_ref, h0_ref, sre0_ref, sim0_ref,
                  y1_ref, conv_out_ref, h_ref, sre_ref, sim_ref,
                  xbuf, abuf, hbuf, bure, buim, gpre, *maybe_xt):
    m = nb * tt
    hist = (LRU_CONV - 1) * nb

    @pl.when(pl.program_id(0) == 0)
    def _():
        xbuf[0:hist, :] = conv0_ref[...]
        h_ref[...] = h0_ref[...]
        sre_ref[...] = sre0_ref[...]
        sim_ref[...] = sim0_ref[...]

    if batch_major:
        xt, x_sems = maybe_xt
        step = pl.program_id(0)
        slot = step % 2

        def tile_copies(tile, into):
            return [pltpu.make_async_copy(x_ref.at[b, pl.ds(tile * tt, tt), :],
                                          xt.at[into, :, b, :],
                                          x_sems.at[into, b]) for b in range(nb)]

        @pl.when(step == 0)
        def _():
            for copy in tile_copies(0, 0):
                copy.start()

        @pl.when(step + 1 < n_tiles)
        def _():
            for copy in tile_copies(step + 1, 1 - slot):
                copy.start()

        for copy in tile_copies(step, slot):
            copy.wait()
        load_x = lambda: xt[slot].reshape(m, D_MODEL)
    else:
        load_x = lambda: x_ref[...]
    xb = load_x().astype(_BF16)

    lam = lam_ref[...]
    c_lam = (0.5 * LRU_C * LOG2_E) * (jnp.minimum(lam, 0.0) - jnp.log1p(jnp.exp(-jnp.abs(lam))))
    gate_cols = D_MODEL + D_S5

    def lru_input_projection(k):
        blk = slice(k * MXU_TILE, (k + 1) * MXU_TILE)
        for r0 in range(0, m, HEAD_ROWS if k == 0 else m):
            rows = slice(r0, r0 + (HEAD_ROWS if k == 0 else m))
            xbuf[hist + rows.start:hist + rows.stop, blk] = _dot(xb[rows, :], w_in_ref[:, blk])

    lru_input_projection(0)
    for k in range(GATE_BLOCKS):
        blk = slice(k * MXU_TILE, (k + 1) * MXU_TILE)
        if k + 1 < GATE_BLOCKS:
            lru_input_projection(k + 1)
        if k == 0:
            u = _dot(xb, w_in_ref[:, D_MODEL:D_MODEL + D_S5])
            ub = u.astype(_BF16)
            for j in range(S5_BLOCKS):
                bu = _dot(ub[:, j * S5_IN_BLOCK:(j + 1) * S5_IN_BLOCK], wb_ref[j])
                cols = slice(j * S5_STATE_BLOCK, (j + 1) * S5_STATE_BLOCK)
                bure[:, cols] = bu[:, 0:S5_STATE_BLOCK]
                buim[:, cols] = bu[:, S5_STATE_BLOCK:]
        xc = cb_ref[:, blk] + xbuf[0:m, blk] * cw_ref[0:1, blk]
        for j in range(1, LRU_CONV):
            xc = xc + xbuf[j * nb:j * nb + m, blk] * cw_ref[j:j + 1, blk]
        new_hist = xbuf[m:m + hist, blk]
        xbuf[0:hist, blk] = new_hist
        conv_out_ref[:, blk] = new_hist
        half_g = _dot(xc.astype(_BF16), wg_ref[k])
        tanh_r = jnp.tanh(half_g[:, 0:MXU_TILE] + ba_ref[:, blk])
        ig = _sigmoid_of_twice(half_g[:, MXU_TILE:] + bx_ref[:, blk])
        a = jnp.exp2(tanh_r * c_lam[:, blk] + c_lam[:, blk])
        abuf[:, blk] = a
        one_minus_a2 = jnp.maximum((1.0 - a) * (1.0 + a), 0.0)
        mult = one_minus_a2 * jax.lax.rsqrt(jnp.maximum(one_minus_a2, RSQRT_FLOOR))
        hbuf[:, blk] = mult * (ig * xc)
        for half in range(2):
            gblk = slice((2 * k + half) * MXU_TILE, (2 * k + half + 1) * MXU_TILE)
            gpre[:, gblk] = _dot(xb, w_in_ref[:, gate_cols + gblk.start:gate_cols + gblk.stop])

    h = h_ref[...]
    for t in range(tt):
        rows = slice(t * nb, (t + 1) * nb)
        h = abuf[rows, :] * h + hbuf[rows, :]
        hbuf[rows, :] = h
    h_ref[...] = h

    ys = []
    for k in range(S5_BLOCKS):
        cols = slice(k * S5_STATE_BLOCK, (k + 1) * S5_STATE_BLOCK)
        ar = jnp.broadcast_to(are_ref[:, cols], (nb, S5_STATE_BLOCK))
        ai = jnp.broadcast_to(aim_ref[:, cols], (nb, S5_STATE_BLOCK))
        hr = sre_ref[:, cols]
        hi = sim_ref[:, cols]
        for t in range(tt):
            rows = slice(t * nb, (t + 1) * nb)
            hr_new = ar * hr - ai * hi + bure[rows, cols]
            hi_new = ar * hi + ai * hr + buim[rows, cols]
            hr, hi = hr_new, hi_new
            bure[rows, cols] = hr
            buim[rows, cols] = hi
        sre_ref[:, cols] = hr
        sim_ref[:, cols] = hi
        ys.append(_dot(bure[:, cols].astype(_BF16), wcr_ref[k])
                  - _dot(buim[:, cols].astype(_BF16), wci_ref[k]))
    y = jnp.concatenate(ys, axis=1) + d_ref[...] * u
    z = jax.nn.gelu(y).astype(_BF16)
    glu = _dot(z, wglu_ref[...])
    s5_out = glu[:, 0:D_MODEL] * _sigmoid_of_twice(glu[:, D_MODEL:])

    g_lru = _sigmoid_of_twice(gpre[:, 0:D_MODEL])
    g_s5 = _sigmoid_of_twice(gpre[:, D_MODEL:])
    merged = (g_lru * hbuf[...] + g_s5 * s5_out).astype(_BF16)
    mix = _dot(merged, wout_ref[...])
    y1_ref[...] = ALPHA * load_x() + mix


def _ffn_kernel(nb, tt, n_tiles, batch_major,
                y1_ref, g1_ref, b1_ref, wup_ref, cw_ref, cb_ref, wdown_ref, g2_ref, b2_ref, conv0_ref,
                out_ref, conv_out_ref,
                abuf, *maybe_yt):
    m = nb * tt
    hist = (FFN_CONV - 1) * nb
    step = pl.program_id(0)

    @pl.when(step == 0)
    def _():
        abuf[0:hist, :] = conv0_ref[...]

    if batch_major:
        yt, y_sems = maybe_yt
        slot = step % 2

        def tile_copies(tile, staged):
            return [pltpu.make_async_copy(yt.at[staged, :, b, :],
                                          out_ref.at[b, pl.ds(tile * tt, tt), :],
                                          y_sems.at[staged, b]) for b in range(nb)]

        @pl.when(step >= 2)
        def _():
            for copy in tile_copies(step - 2, slot):
                copy.wait()

    x = _layer_norm(y1_ref[...], g1_ref[...], b1_ref[...])
    xb = x.astype(_BF16)
    for r0 in range(0, m, HEAD_ROWS):
        abuf[hist + r0:hist + r0 + HEAD_ROWS, 0:D_MODEL] = _dot(xb[r0:r0 + HEAD_ROWS, :], wup_ref[:, 0:D_MODEL])
    abuf[hist:hist + m, D_MODEL:] = _dot(xb, wup_ref[:, D_MODEL:D_FF])
    f = None
    for c in range(D_FF // D_MODEL):
        cols = slice(c * D_MODEL, (c + 1) * D_MODEL)
        ac = cb_ref[:, cols] + abuf[0:m, cols] * cw_ref[0:1, cols]
        for k in range(1, FFN_CONV):
            ac = ac + abuf[k * nb:k * nb + m, cols] * cw_ref[k:k + 1, cols]
        gate = _dot(xb, wup_ref[:, D_FF + c * D_MODEL:D_FF + (c + 1) * D_MODEL])
        hmid = (jax.nn.gelu(ac) * gate).astype(_BF16)
        part = _dot(hmid, wdown_ref[cols, :])
        f = part if f is None else f + part
    new_hist = abuf[m:m + hist, :]
    abuf[0:hist, :] = new_hist
    conv_out_ref[...] = new_hist
    y = _layer_norm(ALPHA * x + f, g2_ref[...], b2_ref[...])
    if batch_major:
        yt[slot] = y.reshape(tt, nb, D_MODEL)
        for copy in tile_copies(step, slot):
            copy.start()

        @pl.when(step == n_tiles - 1)
        def _():
            if n_tiles > 1:
                for copy in tile_copies(step - 1, 1 - slot):
                    copy.wait()
            for copy in tile_copies(step, slot):
                copy.wait()
    else:
        out_ref[...] = y


def _resident(shape):
    zeros = (0,) * len(shape)
    return pl.BlockSpec(shape, lambda i: zeros, pipeline_mode=pl.Buffered(1))


def _run_mixer(x, nb, tt, weights, states, batch_major):
    m = nb * tt
    hist = (LRU_CONV - 1) * nb
    tile = pl.BlockSpec((m, D_MODEL), lambda i: (i, 0))
    if batch_major:
        n_rows = nb * x.shape[1]
        x_spec = pl.BlockSpec(memory_space=pl.ANY)
    else:
        n_rows = x.shape[0]
        x_spec = tile
    operands = (x,) + tuple(weights) + tuple(states)
    in_specs = [x_spec] + [_resident(a.shape) for a in operands[1:]]
    out_shape = (
        jax.ShapeDtypeStruct((n_rows, D_MODEL), _F32),
        jax.ShapeDtypeStruct((hist, D_MODEL), _F32),
        jax.ShapeDtypeStruct((nb, D_MODEL), _F32),
        jax.ShapeDtypeStruct((nb, D_STATE), _F32),
        jax.ShapeDtypeStruct((nb, D_STATE), _F32),
    )
    out_specs = [tile] + [pl.BlockSpec(s.shape, lambda i: (0, 0)) for s in out_shape[1:]]
    scratch_shapes = [
        pltpu.VMEM((hist + m, D_MODEL), _F32),
        pltpu.VMEM((m, D_MODEL), _F32),
        pltpu.VMEM((m, D_MODEL), _F32),
        pltpu.VMEM((m, D_STATE), _F32),
        pltpu.VMEM((m, D_STATE), _F32),
        pltpu.VMEM((m, 2 * D_MODEL), _F32),
    ]
    if batch_major:
        scratch_shapes += [pltpu.VMEM((2, tt, nb, D_MODEL), _F32), pltpu.SemaphoreType.DMA((2, nb))]
    return pl.pallas_call(
        functools.partial(_mixer_kernel, nb, tt, n_rows // m, batch_major),
        out_shape=out_shape,
        grid=(n_rows // m,),
        in_specs=in_specs,
        out_specs=out_specs,
        scratch_shapes=scratch_shapes,
        compiler_params=pltpu.CompilerParams(
            dimension_semantics=("arbitrary",), vmem_limit_bytes=VMEM_LIMIT_BYTES),
        name=f"mixer_nb{nb}",
    )(*operands)


def _run_ffn(x_rows, nb, tt, weights, conv0, batch_major):
    n_rows = x_rows.shape[0]
    m = nb * tt
    hist = (FFN_CONV - 1) * nb
    tile = pl.BlockSpec((m, D_MODEL), lambda i: (i, 0))
    operands = (x_rows,) + tuple(weights) + (conv0,)
    in_specs = [tile] + [_resident(a.shape) for a in operands[1:]]
    if batch_major:
        y_shape = jax.ShapeDtypeStruct((nb, n_rows // nb, D_MODEL), _F32)
        y_spec = pl.BlockSpec(memory_space=pl.ANY)
    else:
        y_shape = jax.ShapeDtypeStruct((n_rows, D_MODEL), _F32)
        y_spec = tile
    out_shape = (y_shape, jax.ShapeDtypeStruct((hist, D_FF), _F32))
    out_specs = [y_spec, pl.BlockSpec((hist, D_FF), lambda i: (0, 0))]
    scratch_shapes = [pltpu.VMEM((hist + m, D_FF), _F32)]
    if batch_major:
        scratch_shapes += [pltpu.VMEM((2, tt, nb, D_MODEL), _F32), pltpu.SemaphoreType.DMA((2, nb))]
    return pl.pallas_call(
        functools.partial(_ffn_kernel, nb, tt, n_rows // m, batch_major),
        out_shape=out_shape,
        grid=(n_rows // m,),
        in_specs=in_specs,
        out_specs=out_specs,
        scratch_shapes=scratch_shapes,
        compiler_params=pltpu.CompilerParams(
            dimension_semantics=("arbitrary",), vmem_limit_bytes=VMEM_LIMIT_BYTES),
        name=f"ffn_nb{nb}",
    )(*operands)


def _block_diag(blocks, n_per_tile):
    n, r, c = blocks.shape
    tiles = n // n_per_tile
    eye = jnp.eye(n_per_tile, dtype=blocks.dtype)
    b = blocks.reshape(tiles, n_per_tile, r, c)
    return jnp.einsum("tgrc,gh->tgrhc", b, eye).reshape(tiles, n_per_tile * r, n_per_tile * c)


def _s5_discretise(a_re, a_im, log_dt, b_re, b_im):
    dt = jnp.exp(log_dt)[:, None]
    mag = jnp.exp(a_re * dt)
    ab_re = mag * jnp.cos(a_im * dt)
    ab_im = mag * jnp.sin(a_im * dt)
    nr = ab_re - 1.0
    ni = ab_im
    den = a_re * a_re + a_im * a_im
    coef_re = (nr * a_re + ni * a_im) / den
    coef_im = (ni * a_re - nr * a_im) / den
    bb_re = coef_re[..., None] * b_re - coef_im[..., None] * b_im
    bb_im = coef_re[..., None] * b_im + coef_im[..., None] * b_re
    return ab_re, ab_im, bb_re, bb_im


def _time_major(x):
    b, l, c = x.shape
    return jnp.transpose(x, (1, 0, 2)).reshape(l * b, c)


def _batch_major(rows, b):
    n, c = rows.shape
    return jnp.transpose(rows.reshape(n // b, b, c), (1, 0, 2))


def _layer(x, lru_conv0, lru_h0, s5_re0, s5_im0, ffn_conv0, tt, mixer_w, ffn_w):
    nb = x.shape[0]
    batch_major = tt % SUBLANES == 0
    states = (_time_major(lru_conv0), lru_h0,
              s5_re0.reshape(nb, D_STATE), s5_im0.reshape(nb, D_STATE))
    x1, conv_out, h_out, sre, sim = _run_mixer(
        x if batch_major else _time_major(x), nb, tt, mixer_w, states, batch_major)
    y, fconv_out = _run_ffn(x1, nb, tt, ffn_w, _time_major(ffn_conv0), batch_major)
    return (y if batch_major else _batch_major(y, nb), _batch_major(conv_out, nb), h_out,
            sre.reshape(nb, S5_GROUPS, S5_STATE), sim.reshape(nb, S5_GROUPS, S5_STATE),
            _batch_major(fconv_out, nb))


def kernel(x_prompt, x_sample, state_lru_conv, state_lru_h, state_s5_re, state_s5_im, state_ffn_conv, w_in, lru_conv_w, lru_conv_b, lru_wa, lru_ba, lru_wx, lru_bx, lru_lambda, s5_a_re, s5_a_im, s5_log_dt, s5_b_re, s5_b_im, s5_c_re, s5_c_im, s5_d, w_glu, w_out, ln1_g, ln1_b, w_up, ffn_conv_w, ffn_conv_b, w_down, ln2_g, ln2_b):
    n_p = x_prompt.shape[0]
    xp, xs = x_prompt, x_sample
    outs_p, outs_s = [], []
    for l in range(DEPTH):
        row = lambda v: v[l].reshape(1, -1)
        heads_per_tile = MXU_TILE // LRU_HEAD_DIM
        wg = (0.5 * jnp.concatenate([_block_diag(lru_wa[l], heads_per_tile),
                                     _block_diag(lru_wx[l], heads_per_tile)], axis=2)).astype(_BF16)
        w_in_scale = jnp.concatenate([jnp.ones((D_MODEL + D_S5,), _F32), jnp.full((2 * D_MODEL,), 0.5, _F32)])
        w_glu_scale = jnp.concatenate([jnp.ones((D_MODEL,), _F32), jnp.full((D_MODEL,), 0.5, _F32)])
        ab_re, ab_im, bb_re, bb_im = _s5_discretise(s5_a_re[l], s5_a_im[l], s5_log_dt[l], s5_b_re[l], s5_b_im[l])
        groups_per_block = S5_IN_BLOCK // S5_GROUP
        to_in = lambda bb: _block_diag(jnp.swapaxes(bb, 1, 2), groups_per_block)
        to_out = lambda cc: _block_diag(jnp.swapaxes(cc, 1, 2), groups_per_block)
        wb = jnp.concatenate([to_in(bb_re), to_in(bb_im)], axis=2).astype(_BF16)
        mixer_w = (
            (w_in[l] * w_in_scale).astype(_BF16), lru_conv_w[l], row(lru_conv_b), wg,
            0.5 * row(lru_ba), 0.5 * row(lru_bx), row(lru_lambda),
            wb, ab_re.reshape(1, D_STATE), ab_im.reshape(1, D_STATE),
            to_out(s5_c_re[l]).astype(_BF16), to_out(s5_c_im[l]).astype(_BF16), row(s5_d),
            (w_glu[l] * w_glu_scale).astype(_BF16), w_out[l].astype(_BF16),
        )
        ffn_w = (row(ln1_g), row(ln1_b), w_up[l].astype(_BF16), ffn_conv_w[l], row(ffn_conv_b), w_down[l].astype(_BF16),
                 row(ln2_g), row(ln2_b))

        zc = jnp.zeros((n_p, LRU_CONV - 1, D_MODEL), _F32)
        zh = jnp.zeros((n_p, D_MODEL), _F32)
        zs = jnp.zeros((n_p, S5_GROUPS, S5_STATE), _F32)
        zf = jnp.zeros((n_p, FFN_CONV - 1, D_FF), _F32)
        xp, *st_p = _layer(xp, zc, zh, zs, zs, zf, PROMPT_TILE_STEPS, mixer_w, ffn_w)
        outs_p.append(st_p)
        xs, *st_s = _layer(xs, state_lru_conv[l], state_lru_h[l], state_s5_re[l], state_s5_im[l],
                           state_ffn_conv[l], xs.shape[1], mixer_w, ffn_w)
        outs_s.append(st_s)
    stack = lambda outs, j: jnp.stack([o[j] for o in outs])
    return (xp, xs,
            *(stack(outs_p, j) for j in range(5)),
            *(stack(outs_s, j) for j in range(5)))
```

```python
import functools
import math

import jax
import jax.numpy as jnp
from jax.experimental import pallas as pl
from jax.experimental.pallas import tpu as pltpu

D_MODEL = 1024
LRU_HEADS = 16
LRU_HEAD_DIM = D_MODEL // LRU_HEADS
LRU_CONV = 4
LRU_C = 8.0
S5_GROUP = 16
D_S5 = D_MODEL // 2
S5_GROUPS = D_S5 // S5_GROUP
S5_STATE = 64
D_STATE = S5_GROUPS * S5_STATE
D_FF = 3 * D_MODEL
FFN_CONV = 3
DEPTH = 1
ALPHA = (2.0 * DEPTH) ** 0.25
LN_EPS = 1e-5
LOG2_E = math.log2(math.e)
RSQRT_FLOOR = 1e-36

MXU_TILE = 256
SUBLANES = 8
LANES = 128
LANE_BLOCKS = D_MODEL // LANES
GATE_BLOCKS = D_MODEL // MXU_TILE
S5_IN_BLOCK = 128
S5_BLOCKS = D_S5 // S5_IN_BLOCK
S5_STATE_BLOCK = D_STATE // S5_BLOCKS
PROMPT_TILE_STEPS = 64
HEAD_ROWS = 256
VMEM_LIMIT_BYTES = 56 * 1024 * 1024

_BF16 = jnp.bfloat16
_F32 = jnp.float32


def _dot(a, b):
    return jnp.dot(a, b, preferred_element_type=_F32)


def _sigmoid_of_twice(half_z):
    return 0.5 * jnp.tanh(half_z) + 0.5


def _layer_norm(y, g, b):
    mu = jnp.mean(y, axis=-1, keepdims=True)
    yc = y - mu
    var = jnp.mean(yc * yc, axis=-1, keepdims=True)
    return yc * jax.lax.rsqrt(var + LN_EPS) * g + b


def _mixer_kernel(nb, tt, n_tiles, batch_major,
                  x_ref, w_in_ref, cw_ref, cb_ref, wg_ref, ba_ref, bx_ref, lam_ref,
                  wb_ref, are_ref, aim_ref, wcr_ref, wci_ref, d_ref, wglu_ref, wout_ref,
                  conv0_ref, h0_ref, sre0_ref, sim0_ref,
                  y1_ref, conv_out_ref, h_ref, sre_ref, sim_ref,
                  xbuf, abuf, hbuf, bure, buim, gpre, *maybe_xt):
    m = nb * tt
    hist = (LRU_CONV - 1) * nb

    @pl.when(pl.program_id(0) == 0)
    def _():
        xbuf[0:hist, :] = conv0_ref[...]
        h_ref[...] = h0_ref[...]
        sre_ref[...] = sre0_ref[...]
        sim_ref[...] = sim0_ref[...]

    if batch_major:
        xt, x_sems = maybe_xt
        step = pl.program_id(0)
        slot = step % 2

        def tile_copies(tile, into):
            return [pltpu.make_async_copy(x_ref.at[b, pl.ds(tile * tt, tt), :],
                                          xt.at[into, :, b, :],
                                          x_sems.at[into, b]) for b in range(nb)]

        @pl.when(step == 0)
        def _():
            for copy in tile_copies(0, 0):
                copy.start()

        @pl.when(step + 1 < n_tiles)
        def _():
            for copy in tile_copies(step + 1, 1 - slot):
                copy.start()

        for copy in tile_copies(step, slot):
            copy.wait()
        load_x = lambda: xt[slot].reshape(m, D_MODEL)
    else:
        load_x = lambda: x_ref[...]
    xb = load_x().astype(_BF16)

    lam = lam_ref[...]
    c_lam = (0.5 * LRU_C * LOG2_E) * (jnp.minimum(lam, 0.0) - jnp.log1p(jnp.exp(-jnp.abs(lam))))
    gate_cols = D_MODEL + D_S5

    def lru_input_projection(k):
        blk = slice(k * MXU_TILE, (k + 1) * MXU_TILE)
        for r0 in range(0, m, HEAD_ROWS if k == 0 else m):
            rows = slice(r0, r0 + (HEAD_ROWS if k == 0 else m))
            xbuf[hist + rows.start:hist + rows.stop, blk] = _dot(xb[rows, :], w_in_ref[:, blk])

    lru_input_projection(0)
    for k in range(GATE_BLOCKS):
        blk = slice(k * MXU_TILE, (k + 1) * MXU_TILE)
        if k + 1 < GATE_BLOCKS:
            lru_input_projection(k + 1)
        if k == 0:
            u = _dot(xb, w_in_ref[:, D_MODEL:D_MODEL + D_S5])
            ub = u.astype(_BF16)
            for j in range(S5_BLOCKS):
                bu = _dot(ub[:, j * S5_IN_BLOCK:(j + 1) * S5_IN_BLOCK], wb_ref[j])
                cols = slice(j * S5_STATE_BLOCK, (j + 1) * S5_STATE_BLOCK)
                bure[:, cols] = bu[:, 0:S5_STATE_BLOCK]
                buim[:, cols] = bu[:, S5_STATE_BLOCK:]
        xc = cb_ref[:, blk] + xbuf[0:m, blk] * cw_ref[0:1, blk]
        for j in range(1, LRU_CONV):
            xc = xc + xbuf[j * nb:j * nb + m, blk] * cw_ref[j:j + 1, blk]
        new_hist = xbuf[m:m + hist, blk]
        xbuf[0:hist, blk] = new_hist
        conv_out_ref[:, blk] = new_hist
        half_g = _dot(xc.astype(_BF16), wg_ref[k])
        tanh_r = jnp.tanh(half_g[:, 0:MXU_TILE] + ba_ref[:, blk])
        ig = _sigmoid_of_twice(half_g[:, MXU_TILE:] + bx_ref[:, blk])
        a = jnp.exp2(tanh_r * c_lam[:, blk] + c_lam[:, blk])
        abuf[:, blk] = a
        one_minus_a2 = jnp.maximum((1.0 - a) * (1.0 + a), 0.0)
        mult = one_minus_a2 * jax.lax.rsqrt(jnp.maximum(one_minus_a2, RSQRT_FLOOR))
        hbuf[:, blk] = mult * (ig * xc)
        for half in range(2):
            gblk = slice((2 * k + half) * MXU_TILE, (2 * k + half + 1) * MXU_TILE)
            gpre[:, gblk] = _dot(xb, w_in_ref[:, gate_cols + gblk.start:gate_cols + gblk.stop])

    h = h_ref[...]
    for t in range(tt):
        rows = slice(t * nb, (t + 1) * nb)
        h = abuf[rows, :] * h + hbuf[rows, :]
        hbuf[rows, :] = h
    h_ref[...] = h

    ys = []
    for k in range(S5_BLOCKS):
        cols = slice(k * S5_STATE_BLOCK, (k + 1) * S5_STATE_BLOCK)
        ar = jnp.broadcast_to(are_ref[:, cols], (nb, S5_STATE_BLOCK))
        ai = jnp.broadcast_to(aim_ref[:, cols], (nb, S5_STATE_BLOCK))
        hr = sre_ref[:, cols]
        hi = sim_ref[:, cols]
        for t in range(tt):
            rows = slice(t * nb, (t + 1) * nb)
            hr_new = ar * hr - ai * hi + bure[rows, cols]
            hi_new = ar * hi + ai * hr + buim[rows, cols]
            hr, hi = hr_new, hi_new
            bure[rows, cols] = hr
            buim[rows, cols] = hi
        sre_ref[:, cols] = hr
        sim_ref[:, cols] = hi
        ys.append(_dot(bure[:, cols].astype(_BF16), wcr_ref[k])
                  - _dot(buim[:, cols].astype(_BF16), wci_ref[k]))
    y = jnp.concatenate(ys, axis=1) + d_ref[...] * u
    z = jax.nn.gelu(y).astype(_BF16)
    glu = _dot(z, wglu_ref[...])
    s5_out = glu[:, 0:D_MODEL] * _sigmoid_of_twice(glu[:, D_MODEL:])

    g_lru = _sigmoid_of_twice(gpre[:, 0:D_MODEL])
    g_s5 = _sigmoid_of_twice(gpre[:, D_MODEL:])
    merged = (g_lru * hbuf[...] + g_s5 * s5_out).astype(_BF16)
    mix = _dot(merged, wout_ref[...])
    y1_ref[...] = ALPHA * load_x() + mix


def _ffn_kernel(nb, tt, n_tiles, batch_major,
                y1_ref, g1_ref, b1_ref, wup_ref, cw_ref, cb_ref, wdown_ref, g2_ref, b2_ref, conv0_ref,
                out_ref, conv_out_ref,
                abuf, *maybe_yt):
    m = nb * tt
    hist = (FFN_CONV - 1) * nb
    step = pl.program_id(0)

    @pl.when(step == 0)
    def _():
        abuf[0:hist, :] = conv0_ref[...]

    if batch_major:
        yt, y_sems = maybe_yt
        slot = step % 2

        def tile_copies(tile, staged):
            return [pltpu.make_async_copy(yt.at[staged, :, b, :],
                                          out_ref.at[b, pl.ds(tile * tt, tt), :],
                                          y_sems.at[staged, b]) for b in range(nb)]

        @pl.when(step >= 2)
        def _():
            for copy in tile_copies(step - 2, slot):
                copy.wait()

    x = _layer_norm(y1_ref[...], g1_ref[...], b1_ref[...])
    xb = x.astype(_BF16)
    for r0 in range(0, m, HEAD_ROWS):
        abuf[hist + r0:hist + r0 + HEAD_ROWS, 0:D_MODEL] = _dot(xb[r0:r0 + HEAD_ROWS, :], wup_ref[:, 0:D_MODEL])
    abuf[hist:hist + m, D_MODEL:] = _dot(xb, wup_ref[:, D_MODEL:D_FF])
    f = None
    for c in range(D_FF // D_MODEL):
        cols = slice(c * D_MODEL, (c + 1) * D_MODEL)
        ac = cb_ref[:, cols] + abuf[0:m, cols] * cw_ref[0:1, cols]
        for k in range(1, FFN_CONV):
            ac = ac + abuf[k * nb:k * nb + m, cols] * cw_ref[k:k + 1, cols]
        gate = _dot(xb, wup_ref[:, D_FF + c * D_MODEL:D_FF + (c + 1) * D_MODEL])
        hmid = (jax.nn.gelu(ac) * gate).astype(_BF16)
        part = _dot(hmid, wdown_ref[cols, :])
        f = part if f is None else f + part
    new_hist = abuf[m:m + hist, :]
    abuf[0:hist, :] = new_hist
    conv_out_ref[...] = new_hist
    y = _layer_norm(ALPHA * x + f, g2_ref[...], b2_ref[...])
    if batch_major:
        yt[slot] = y.reshape(tt, nb, D_MODEL)
        for copy in tile_copies(step, slot):
            copy.start()

        @pl.when(step == n_tiles - 1)
        def _():
            if n_tiles > 1:
                for copy in tile_copies(step - 1, 1 - slot):
                    copy.wait()
            for copy in tile_copies(step, slot):
                copy.wait()
    else:
        out_ref[...] = y


def _resident(shape):
    zeros = (0,) * len(shape)
    return pl.BlockSpec(shape, lambda i: zeros, pipeline_mode=pl.Buffered(1))


def _run_mixer(x, nb, tt, weights, states, batch_major):
    m = nb * tt
    hist = (LRU_CONV - 1) * nb
    tile = pl.BlockSpec((m, D_MODEL), lambda i: (i, 0))
    if batch_major:
        n_rows = nb * x.shape[1]
        x_spec = pl.BlockSpec(memory_space=pl.ANY)
    else:
        n_rows = x.shape[0]
        x_spec = tile
    operands = (x,) + tuple(weights) + tuple(states)
    in_specs = [x_spec] + [_resident(a.shape) for a in operands[1:]]
    out_shape = (
        jax.ShapeDtypeStruct((n_rows, D_MODEL), _F32),
        jax.ShapeDtypeStruct((hist, D_MODEL), _F32),
        jax.ShapeDtypeStruct((nb, D_MODEL), _F32),
        jax.ShapeDtypeStruct((nb, D_STATE), _F32),
        jax.ShapeDtypeStruct((nb, D_STATE), _F32),
    )
    out_specs = [tile] + [pl.BlockSpec(s.shape, lambda i: (0, 0)) for s in out_shape[1:]]
    scratch_shapes = [
        pltpu.VMEM((hist + m, D_MODEL), _F32),
        pltpu.VMEM((m, D_MODEL), _F32),
        pltpu.VMEM((m, D_MODEL), _F32),
        pltpu.VMEM((m, D_STATE), _F32),
        pltpu.VMEM((m, D_STATE), _F32),
        pltpu.VMEM((m, 2 * D_MODEL), _F32),
    ]
    if batch_major:
        scratch_shapes += [pltpu.VMEM((2, tt, nb, D_MODEL), _F32), pltpu.SemaphoreType.DMA((2, nb))]
    return pl.pallas_call(
        functools.partial(_mixer_kernel, nb, tt, n_rows // m, batch_major),
        out_shape=out_shape,
        grid=(n_rows // m,),
        in_specs=in_specs,
        out_specs=out_specs,
        scratch_shapes=scratch_shapes,
        compiler_params=pltpu.CompilerParams(
            dimension_semantics=("arbitrary",), vmem_limit_bytes=VMEM_LIMIT_BYTES),
        name=f"mixer_nb{nb}",
    )(*operands)


def _run_ffn(x_rows, nb, tt, weights, conv0, batch_major):
    n_rows = x_rows.shape[0]
    m = nb * tt
    hist = (FFN_CONV - 1) * nb
    tile = pl.BlockSpec((m, D_MODEL), lambda i: (i, 0))
    operands = (x_rows,) + tuple(weights) + (conv0,)
    in_specs = [tile] + [_resident(a.shape) for a in operands[1:]]
    if batch_major:
        y_shape = jax.ShapeDtypeStruct((nb, n_rows // nb, D_MODEL), _F32)
        y_spec = pl.BlockSpec(memory_space=pl.ANY)
    else:
        y_shape = jax.ShapeDtypeStruct((n_rows, D_MODEL), _F32)
        y_spec = tile
    out_shape = (y_shape, jax.ShapeDtypeStruct((hist, D_FF), _F32))
    out_specs = [y_spec, pl.BlockSpec((hist, D_FF), lambda i: (0, 0))]
    scratch_shapes = [pltpu.VMEM((hist + m, D_FF), _F32)]
    if batch_major:
        scratch_shapes += [pltpu.VMEM((2, tt, nb, D_MODEL), _F32), pltpu.SemaphoreType.DMA((2, nb))]
    return pl.pallas_call(
        functools.partial(_ffn_kernel, nb, tt, n_rows // m, batch_major),
        out_shape=out_shape,
        grid=(n_rows // m,),
        in_specs=in_specs,
        out_specs=out_specs,
        scratch_shapes=scratch_shapes,
        compiler_params=pltpu.CompilerParams(
            dimension_semantics=("arbitrary",), vmem_limit_bytes=VMEM_LIMIT_BYTES),
        name=f"ffn_nb{nb}",
    )(*operands)


def _block_diag(blocks, n_per_tile):
    n, r, c = blocks.shape
    tiles = n // n_per_tile
    eye = jnp.eye(n_per_tile, dtype=blocks.dtype)
    b = blocks.reshape(tiles, n_per_tile, r, c)
    return jnp.einsum("tgrc,gh->tgrhc", b, eye).reshape(tiles, n_per_tile * r, n_per_tile * c)


def _s5_discretise(a_re, a_im, log_dt, b_re, b_im):
    dt = jnp.exp(log_dt)[:, None]
    mag = jnp.exp(a_re * dt)
    ab_re = mag * jnp.cos(a_im * dt)
    ab_im = mag * jnp.sin(a_im * dt)
    nr = ab_re - 1.0
    ni = ab_im
    den = a_re * a_re + a_im * a_im
    coef_re = (nr * a_re + ni * a_im) / den
    coef_im = (ni * a_re - nr * a_im) / den
    bb_re = coef_re[..., None] * b_re - coef_im[..., None] * b_im
    bb_im = coef_re[..., None] * b_im + coef_im[..., None] * b_re
    return ab_re, ab_im, bb_re, bb_im


def _time_major(x):
    b, l, c = x.shape
    return jnp.transpose(x, (1, 0, 2)).reshape(l * b, c)


def _batch_major(rows, b):
    n, c = rows.shape
    return jnp.transpose(rows.reshape(n // b, b, c), (1, 0, 2))


def _layer(x, lru_conv0, lru_h0, s5_re0, s5_im0, ffn_conv0, tt, mixer_w, ffn_w):
    nb = x.shape[0]
    batch_major = nb % SUBLANES == 0
    states = (_time_major(lru_conv0), lru_h0,
              s5_re0.reshape(nb, D_STATE), s5_im0.reshape(nb, D_STATE))
    x1, conv_out, h_out, sre, sim = _run_mixer(
        x if batch_major else _time_major(x), nb, tt, mixer_w, states, batch_major)
    y, fconv_out = _run_ffn(x1, nb, tt, ffn_w, _time_major(ffn_conv0), batch_major)
    return (y if batch_major else _batch_major(y, nb), _batch_major(conv_out, nb), h_out,
            sre.reshape(nb, S5_GROUPS, S5_STATE), sim.reshape(nb, S5_GROUPS, S5_STATE),
            _batch_major(fconv_out, nb))


def kernel(x_prompt, x_sample, state_lru_conv, state_lru_h, state_s5_re, state_s5_im, state_ffn_conv, w_in, lru_conv_w, lru_conv_b, lru_wa, lru_ba, lru_wx, lru_bx, lru_lambda, s5_a_re, s5_a_im, s5_log_dt, s5_b_re, s5_b_im, s5_c_re, s5_c_im, s5_d, w_glu, w_out, ln1_g, ln1_b, w_up, ffn_conv_w, ffn_conv_b, w_down, ln2_g, ln2_b):
    n_p = x_prompt.shape[0]
    xp, xs = x_prompt, x_sample
    outs_p, outs_s = [], []
    for l in range(DEPTH):
        row = lambda v: v[l].reshape(1, -1)
        heads_per_tile = MXU_TILE // LRU_HEAD_DIM
        wg = (0.5 * jnp.concatenate([_block_diag(lru_wa[l], heads_per_tile),
                                     _block_diag(lru_wx[l], heads_per_tile)], axis=2)).astype(_BF16)
        w_in_scale = jnp.concatenate([jnp.ones((D_MODEL + D_S5,), _F32), jnp.full((2 * D_MODEL,), 0.5, _F32)])
        w_glu_scale = jnp.concatenate([jnp.ones((D_MODEL,), _F32), jnp.full((D_MODEL,), 0.5, _F32)])
        ab_re, ab_im, bb_re, bb_im = _s5_discretise(s5_a_re[l], s5_a_im[l], s5_log_dt[l], s5_b_re[l], s5_b_im[l])
        groups_per_block = S5_IN_BLOCK // S5_GROUP
        to_in = lambda bb: _block_diag(jnp.swapaxes(bb, 1, 2), groups_per_block)
        to_out = lambda cc: _block_diag(jnp.swapaxes(cc, 1, 2), groups_per_block)
        wb = jnp.concatenate([to_in(bb_re), to_in(bb_im)], axis=2).astype(_BF16)
        mixer_w = (
            (w_in[l] * w_in_scale).astype(_BF16), lru_conv_w[l], row(lru_conv_b), wg,
            0.5 * row(lru_ba), 0.5 * row(lru_bx), row(lru_lambda),
            wb, ab_re.reshape(1, D_STATE), ab_im.reshape(1, D_STATE),
            to_out(s5_c_re[l]).astype(_BF16), to_out(s5_c_im[l]).astype(_BF16), row(s5_d),
            (w_glu[l] * w_glu_scale).astype(_BF16), w_out[l].astype(_BF16),
        )
        ffn_w = (row(ln1_g), row(ln1_b), w_up[l].astype(_BF16), ffn_conv_w[l], row(ffn_conv_b), w_down[l].astype(_BF16),
                 row(ln2_g), row(ln2_b))

        zc = jnp.zeros((n_p, LRU_CONV - 1, D_MODEL), _F32)
        zh = jnp.zeros((n_p, D_MODEL), _F32)
        zs = jnp.zeros((n_p, S5_GROUPS, S5_STATE), _F32)
        zf = jnp.zeros((n_p, FFN_CONV - 1, D_FF), _F32)
        xp, *st_p = _layer(xp, zc, zh, zs, zs, zf, PROMPT_TILE_STEPS, mixer_w, ffn_w)
        outs_p.append(st_p)
        xs, *st_s = _layer(xs, state_lru_conv[l], state_lru_h[l], state_s5_re[l], state_s5_im[l],
                           state_ffn_conv[l], xs.shape[1], mixer_w, ffn_w)
        outs_s.append(st_s)
    stack = lambda outs, j: jnp.stack([o[j] for o in outs])
    return (xp, xs,
            *(stack(outs_p, j) for j in range(5)),
            *(stack(outs_s, j) for j in range(5)))
```

```python
import functools
import math

import jax
import jax.numpy as jnp
from jax.experimental import pallas as pl
from jax.experimental.pallas import tpu as pltpu

D_MODEL = 1024
LRU_HEADS = 16
LRU_HEAD_DIM = D_MODEL // LRU_HEADS
LRU_CONV = 4
LRU_C = 8.0
S5_GROUP = 16
D_S5 = D_MODEL // 2
S5_GROUPS = D_S5 // S5_GROUP
S5_STATE = 64
D_STATE = S5_GROUPS * S5_STATE
D_FF = 3 * D_MODEL
FFN_CONV = 3
DEPTH = 1
ALPHA = (2.0 * DEPTH) ** 0.25
LN_EPS = 1e-5
LOG2_E = math.log2(math.e)
RSQRT_FLOOR = 1e-36

MXU_TILE = 256
SUBLANES = 8
LANES = 128
LANE_BLOCKS = D_MODEL // LANES
GATE_BLOCKS = D_MODEL // MXU_TILE
S5_IN_BLOCK = 128
S5_BLOCKS = D_S5 // S5_IN_BLOCK
S5_STATE_BLOCK = D_STATE // S5_BLOCKS
PROMPT_TILE_STEPS = 64
HEAD_ROWS = 256
FFN_W_UP, FFN_W_DOWN = 2, 5
CAST_CHUNKS_UP, CAST_CHUNKS_DOWN = 16, 8
VMEM_LIMIT_BYTES = 56 * 1024 * 1024

_BF16 = jnp.bfloat16
_F32 = jnp.float32


def _dot(a, b):
    return jnp.dot(a, b, preferred_element_type=_F32)


def _sigmoid_of_twice(half_z):
    return 0.5 * jnp.tanh(half_z) + 0.5


def _layer_norm(y, g, b):
    mu = jnp.mean(y, axis=-1, keepdims=True)
    yc = y - mu
    var = jnp.mean(yc * yc, axis=-1, keepdims=True)
    return yc * jax.lax.rsqrt(var + LN_EPS) * g + b


def _cast_weight_to_bf16(src_hbm, dst_vmem, out_hbm, stage, in_sem, out_sems, first_out_sem):
    chunk = stage.shape[0]
    out_copies = []
    for c in range(src_hbm.shape[0] // chunk):
        rows = pl.ds(c * chunk, chunk)
        load = pltpu.make_async_copy(src_hbm.at[rows, :], stage, in_sem.at[0])
        load.start()
        load.wait()
        dst_vmem[rows, :] = stage[...].astype(_BF16)
        out_copy = pltpu.make_async_copy(dst_vmem.at[rows, :], out_hbm.at[rows, :], out_sems.at[first_out_sem + c])
        out_copy.start()
        out_copies.append(out_copy)
    return out_copies


def _mixer_kernel(nb, tt, n_tiles, batch_major,
                  x_ref, w_in_ref, cw_ref, cb_ref, wg_ref, ba_ref, bx_ref, lam_ref,
                  wb_ref, are_ref, aim_ref, wcr_ref, wci_ref, d_ref, wglu_ref, wout_ref,
                  conv0_ref, h0_ref, sre0_ref, sim0_ref,
                  y1_ref, conv_out_ref, h_ref, sre_ref, sim_ref,
                  xbuf, abuf, hbuf, bure, buim, gpre, *maybe_xt):
    m = nb * tt
    hist = (LRU_CONV - 1) * nb

    @pl.when(pl.program_id(0) == 0)
    def _():
        xbuf[0:hist, :] = conv0_ref[...]
        h_ref[...] = h0_ref[...]
        sre_ref[...] = sre0_ref[...]
        sim_ref[...] = sim0_ref[...]

    if batch_major:
        xt, x_sems = maybe_xt
        step = pl.program_id(0)
        slot = step % 2

        def tile_copies(tile, into):
            return [pltpu.make_async_copy(x_ref.at[b, pl.ds(tile * tt, tt), :],
                                          xt.at[into, :, b, :],
                                          x_sems.at[into, b]) for b in range(nb)]

        @pl.when(step == 0)
        def _():
            for copy in tile_copies(0, 0):
                copy.start()

        @pl.when(step + 1 < n_tiles)
        def _():
            for copy in tile_copies(step + 1, 1 - slot):
                copy.start()

        for copy in tile_copies(step, slot):
            copy.wait()
        load_x = lambda: xt[slot].reshape(m, D_MODEL)
    else:
        load_x = lambda: x_ref[...]
    xb = load_x().astype(_BF16)

    lam = lam_ref[...]
    c_lam = (0.5 * LRU_C * LOG2_E) * (jnp.minimum(lam, 0.0) - jnp.log1p(jnp.exp(-jnp.abs(lam))))
    gate_cols = D_MODEL + D_S5

    def lru_input_projection(k):
        blk = slice(k * MXU_TILE, (k + 1) * MXU_TILE)
        for r0 in range(0, m, HEAD_ROWS if k == 0 else m):
            rows = slice(r0, r0 + (HEAD_ROWS if k == 0 else m))
            xbuf[hist + rows.start:hist + rows.stop, blk] = _dot(xb[rows, :], w_in_ref[:, blk])

    lru_input_projection(0)
    for k in range(GATE_BLOCKS):
        blk = slice(k * MXU_TILE, (k + 1) * MXU_TILE)
        if k + 1 < GATE_BLOCKS:
            lru_input_projection(k + 1)
        if k == 0:
            u = _dot(xb, w_in_ref[:, D_MODEL:D_MODEL + D_S5])
            ub = u.astype(_BF16)
            for j in range(S5_BLOCKS):
                bu = _dot(ub[:, j * S5_IN_BLOCK:(j + 1) * S5_IN_BLOCK], wb_ref[j])
                cols = slice(j * S5_STATE_BLOCK, (j + 1) * S5_STATE_BLOCK)
                bure[:, cols] = bu[:, 0:S5_STATE_BLOCK]
                buim[:, cols] = bu[:, S5_STATE_BLOCK:]
        xc = cb_ref[:, blk] + xbuf[0:m, blk] * cw_ref[0:1, blk]
        for j in range(1, LRU_CONV):
            xc = xc + xbuf[j * nb:j * nb + m, blk] * cw_ref[j:j + 1, blk]
        new_hist = xbuf[m:m + hist, blk]
        xbuf[0:hist, blk] = new_hist
        conv_out_ref[:, blk] = new_hist
        half_g = _dot(xc.astype(_BF16), wg_ref[k])
        tanh_r = jnp.tanh(half_g[:, 0:MXU_TILE] + ba_ref[:, blk])
        ig = _sigmoid_of_twice(half_g[:, MXU_TILE:] + bx_ref[:, blk])
        a = jnp.exp2(tanh_r * c_lam[:, blk] + c_lam[:, blk])
        abuf[:, blk] = a
        one_minus_a2 = jnp.maximum((1.0 - a) * (1.0 + a), 0.0)
        mult = one_minus_a2 * jax.lax.rsqrt(jnp.maximum(one_minus_a2, RSQRT_FLOOR))
        hbuf[:, blk] = mult * (ig * xc)
        for half in range(2):
            gblk = slice((2 * k + half) * MXU_TILE, (2 * k + half + 1) * MXU_TILE)
            gpre[:, gblk] = _dot(xb, w_in_ref[:, gate_cols + gblk.start:gate_cols + gblk.stop])

    h = h_ref[...]
    for t in range(tt):
        rows = slice(t * nb, (t + 1) * nb)
        h = abuf[rows, :] * h + hbuf[rows, :]
        hbuf[rows, :] = h
    h_ref[...] = h

    ys = []
    for k in range(S5_BLOCKS):
        cols = slice(k * S5_STATE_BLOCK, (k + 1) * S5_STATE_BLOCK)
        ar = jnp.broadcast_to(are_ref[:, cols], (nb, S5_STATE_BLOCK))
        ai = jnp.broadcast_to(aim_ref[:, cols], (nb, S5_STATE_BLOCK))
        hr = sre_ref[:, cols]
        hi = sim_ref[:, cols]
        for t in range(tt):
            rows = slice(t * nb, (t + 1) * nb)
            hr_new = ar * hr - ai * hi + bure[rows, cols]
            hi_new = ar * hi + ai * hr + buim[rows, cols]
            hr, hi = hr_new, hi_new
            bure[rows, cols] = hr
            buim[rows, cols] = hi
        sre_ref[:, cols] = hr
        sim_ref[:, cols] = hi
        ys.append(_dot(bure[:, cols].astype(_BF16), wcr_ref[k])
                  - _dot(buim[:, cols].astype(_BF16), wci_ref[k]))
    y = jnp.concatenate(ys, axis=1) + d_ref[...] * u
    z = jax.nn.gelu(y).astype(_BF16)
    glu = _dot(z, wglu_ref[...])
    s5_out = glu[:, 0:D_MODEL] * _sigmoid_of_twice(glu[:, D_MODEL:])

    g_lru = _sigmoid_of_twice(gpre[:, 0:D_MODEL])
    g_s5 = _sigmoid_of_twice(gpre[:, D_MODEL:])
    merged = (g_lru * hbuf[...] + g_s5 * s5_out).astype(_BF16)
    mix = _dot(merged, wout_ref[...])
    y1_ref[...] = ALPHA * load_x() + mix


def _ffn_kernel(nb, tt, n_tiles, batch_major, cast_weights,
                y1_ref, g1_ref, b1_ref, wup_ref, cw_ref, cb_ref, wdown_ref, g2_ref, b2_ref, conv0_ref,
                out_ref, conv_out_ref, *rest):
    m = nb * tt
    hist = (FFN_CONV - 1) * nb
    step = pl.program_id(0)
    weight_copies = []
    if cast_weights:
        assert n_tiles == 1
        (wup_out, wdown_out, abuf, wup_b, wdown_b, up_stage, down_stage, in_sem, out_sems, *maybe_yt) = rest
        n_up = wup_ref.shape[0] // up_stage.shape[0]
        weight_copies += _cast_weight_to_bf16(wup_ref, wup_b, wup_out, up_stage, in_sem, out_sems, 0)
        weight_copies += _cast_weight_to_bf16(wdown_ref, wdown_b, wdown_out, down_stage, in_sem, out_sems, n_up)
        wup_ref, wdown_ref = wup_b, wdown_b
    else:
        (abuf, *maybe_yt) = rest

    @pl.when(step == 0)
    def _():
        abuf[0:hist, :] = conv0_ref[...]

    if batch_major:
        yt, y_sems = maybe_yt
        slot = step % 2

        def tile_copies(tile, staged):
            return [pltpu.make_async_copy(yt.at[staged, :, b, :],
                                          out_ref.at[b, pl.ds(tile * tt, tt), :],
                                          y_sems.at[staged, b]) for b in range(nb)]

        @pl.when(step >= 2)
        def _():
            for copy in tile_copies(step - 2, slot):
                copy.wait()

    x = _layer_norm(y1_ref[...], g1_ref[...], b1_ref[...])
    xb = x.astype(_BF16)
    for r0 in range(0, m, HEAD_ROWS):
        abuf[hist + r0:hist + r0 + HEAD_ROWS, 0:D_MODEL] = _dot(xb[r0:r0 + HEAD_ROWS, :], wup_ref[:, 0:D_MODEL])
    abuf[hist:hist + m, D_MODEL:] = _dot(xb, wup_ref[:, D_MODEL:D_FF])
    f = None
    for c in range(D_FF // D_MODEL):
        cols = slice(c * D_MODEL, (c + 1) * D_MODEL)
        ac = cb_ref[:, cols] + abuf[0:m, cols] * cw_ref[0:1, cols]
        for k in range(1, FFN_CONV):
            ac = ac + abuf[k * nb:k * nb + m, cols] * cw_ref[k:k + 1, cols]
        gate = _dot(xb, wup_ref[:, D_FF + c * D_MODEL:D_FF + (c + 1) * D_MODEL])
        hmid = (jax.nn.gelu(ac) * gate).astype(_BF16)
        part = _dot(hmid, wdown_ref[cols, :])
        f = part if f is None else f + part
    new_hist = abuf[m:m + hist, :]
    abuf[0:hist, :] = new_hist
    conv_out_ref[...] = new_hist
    y = _layer_norm(ALPHA * x + f, g2_ref[...], b2_ref[...])
    if batch_major:
        yt[slot] = y.reshape(tt, nb, D_MODEL)
        for copy in tile_copies(step, slot):
            copy.start()

        @pl.when(step == n_tiles - 1)
        def _():
            if n_tiles > 1:
                for copy in tile_copies(step - 1, 1 - slot):
                    copy.wait()
            for copy in tile_copies(step, slot):
                copy.wait()
    else:
        out_ref[...] = y
    for copy in weight_copies:
        copy.wait()


def _resident(shape):
    zeros = (0,) * len(shape)
    return pl.BlockSpec(shape, lambda i: zeros, pipeline_mode=pl.Buffered(1))


def _run_mixer(x, nb, tt, weights, states, batch_major):
    m = nb * tt
    hist = (LRU_CONV - 1) * nb
    tile = pl.BlockSpec((m, D_MODEL), lambda i: (i, 0))
    if batch_major:
        n_rows = nb * x.shape[1]
        x_spec = pl.BlockSpec(memory_space=pl.ANY)
    else:
        n_rows = x.shape[0]
        x_spec = tile
    operands = (x,) + tuple(weights) + tuple(states)
    in_specs = [x_spec] + [_resident(a.shape) for a in operands[1:]]
    out_shape = (
        jax.ShapeDtypeStruct((n_rows, D_MODEL), _F32),
        jax.ShapeDtypeStruct((hist, D_MODEL), _F32),
        jax.ShapeDtypeStruct((nb, D_MODEL), _F32),
        jax.ShapeDtypeStruct((nb, D_STATE), _F32),
        jax.ShapeDtypeStruct((nb, D_STATE), _F32),
    )
    out_specs = [tile] + [pl.BlockSpec(s.shape, lambda i: (0, 0)) for s in out_shape[1:]]
    scratch_shapes = [
        pltpu.VMEM((hist + m, D_MODEL), _F32),
        pltpu.VMEM((m, D_MODEL), _F32),
        pltpu.VMEM((m, D_MODEL), _F32),
        pltpu.VMEM((m, D_STATE), _F32),
        pltpu.VMEM((m, D_STATE), _F32),
        pltpu.VMEM((m, 2 * D_MODEL), _F32),
    ]
    if batch_major:
        scratch_shapes += [pltpu.VMEM((2, tt, nb, D_MODEL), _F32), pltpu.SemaphoreType.DMA((2, nb))]
    return pl.pallas_call(
        functools.partial(_mixer_kernel, nb, tt, n_rows // m, batch_major),
        out_shape=out_shape,
        grid=(n_rows // m,),
        in_specs=in_specs,
        out_specs=out_specs,
        scratch_shapes=scratch_shapes,
        compiler_params=pltpu.CompilerParams(
            dimension_semantics=("arbitrary",), vmem_limit_bytes=VMEM_LIMIT_BYTES),
        name=f"mixer_nb{nb}",
    )(*operands)


def _run_ffn(x_rows, nb, tt, weights, conv0, batch_major, cast_weights):
    n_rows = x_rows.shape[0]
    m = nb * tt
    hist = (FFN_CONV - 1) * nb
    tile = pl.BlockSpec((m, D_MODEL), lambda i: (i, 0))
    in_hbm = pl.BlockSpec(memory_space=pl.ANY)
    operands = (x_rows,) + tuple(weights) + (conv0,)
    in_specs = [tile] + [_resident(a.shape) for a in operands[1:]]
    if batch_major:
        y_shape = jax.ShapeDtypeStruct((nb, n_rows // nb, D_MODEL), _F32)
        y_spec = in_hbm
    else:
        y_shape = jax.ShapeDtypeStruct((n_rows, D_MODEL), _F32)
        y_spec = tile
    out_shape = [y_shape, jax.ShapeDtypeStruct((hist, D_FF), _F32)]
    out_specs = [y_spec, pl.BlockSpec((hist, D_FF), lambda i: (0, 0))]
    scratch_shapes = [pltpu.VMEM((hist + m, D_FF), _F32)]
    if cast_weights:
        w_up, w_down = weights[FFN_W_UP], weights[FFN_W_DOWN]
        in_specs[1 + FFN_W_UP] = in_hbm
        in_specs[1 + FFN_W_DOWN] = in_hbm
        out_shape += [jax.ShapeDtypeStruct(w_up.shape, _BF16), jax.ShapeDtypeStruct(w_down.shape, _BF16)]
        out_specs += [in_hbm, in_hbm]
        up_chunk, down_chunk = w_up.shape[0] // CAST_CHUNKS_UP, w_down.shape[0] // CAST_CHUNKS_DOWN
        scratch_shapes += [
            pltpu.VMEM(w_up.shape, _BF16), pltpu.VMEM(w_down.shape, _BF16),
            pltpu.VMEM((up_chunk, w_up.shape[1]), _F32), pltpu.VMEM((down_chunk, w_down.shape[1]), _F32),
            pltpu.SemaphoreType.DMA((1,)), pltpu.SemaphoreType.DMA((CAST_CHUNKS_UP + CAST_CHUNKS_DOWN,)),
        ]
    if batch_major:
        scratch_shapes += [pltpu.VMEM((2, tt, nb, D_MODEL), _F32), pltpu.SemaphoreType.DMA((2, nb))]
    return pl.pallas_call(
        functools.partial(_ffn_kernel, nb, tt, n_rows // m, batch_major, cast_weights),
        out_shape=tuple(out_shape),
        grid=(n_rows // m,),
        in_specs=in_specs,
        out_specs=out_specs,
        scratch_shapes=scratch_shapes,
        compiler_params=pltpu.CompilerParams(
            dimension_semantics=("arbitrary",), vmem_limit_bytes=VMEM_LIMIT_BYTES),
        name=f"ffn_nb{nb}",
    )(*operands)


def _block_diag(blocks, n_per_tile):
    n, r, c = blocks.shape
    tiles = n // n_per_tile
    eye = jnp.eye(n_per_tile, dtype=blocks.dtype)
    b = blocks.reshape(tiles, n_per_tile, r, c)
    return jnp.einsum("tgrc,gh->tgrhc", b, eye).reshape(tiles, n_per_tile * r, n_per_tile * c)


def _s5_discretise(a_re, a_im, log_dt, b_re, b_im):
    dt = jnp.exp(log_dt)[:, None]
    mag = jnp.exp(a_re * dt)
    ab_re = mag * jnp.cos(a_im * dt)
    ab_im = mag * jnp.sin(a_im * dt)
    nr = ab_re - 1.0
    ni = ab_im
    den = a_re * a_re + a_im * a_im
    coef_re = (nr * a_re + ni * a_im) / den
    coef_im = (ni * a_re - nr * a_im) / den
    bb_re = coef_re[..., None] * b_re - coef_im[..., None] * b_im
    bb_im = coef_re[..., None] * b_im + coef_im[..., None] * b_re
    return ab_re, ab_im, bb_re, bb_im


def _time_major(x):
    b, l, c = x.shape
    return jnp.transpose(x, (1, 0, 2)).reshape(l * b, c)


def _batch_major(rows, b):
    n, c = rows.shape
    return jnp.transpose(rows.reshape(n // b, b, c), (1, 0, 2))


def _layer(x, lru_conv0, lru_h0, s5_re0, s5_im0, ffn_conv0, tt, mixer_w, ffn_w, cast_ffn_weights=False):
    nb = x.shape[0]
    batch_major = tt % SUBLANES == 0
    states = (_time_major(lru_conv0), lru_h0,
              s5_re0.reshape(nb, D_STATE), s5_im0.reshape(nb, D_STATE))
    x1, conv_out, h_out, sre, sim = _run_mixer(
        x if batch_major else _time_major(x), nb, tt, mixer_w, states, batch_major)
    y, fconv_out, *cast = _run_ffn(x1, nb, tt, ffn_w, _time_major(ffn_conv0), batch_major, cast_ffn_weights)
    outs = (y if batch_major else _batch_major(y, nb), _batch_major(conv_out, nb), h_out,
            sre.reshape(nb, S5_GROUPS, S5_STATE), sim.reshape(nb, S5_GROUPS, S5_STATE),
            _batch_major(fconv_out, nb))
    return (outs, tuple(cast)) if cast_ffn_weights else outs


def kernel(x_prompt, x_sample, state_lru_conv, state_lru_h, state_s5_re, state_s5_im, state_ffn_conv, w_in, lru_conv_w, lru_conv_b, lru_wa, lru_ba, lru_wx, lru_bx, lru_lambda, s5_a_re, s5_a_im, s5_log_dt, s5_b_re, s5_b_im, s5_c_re, s5_c_im, s5_d, w_glu, w_out, ln1_g, ln1_b, w_up, ffn_conv_w, ffn_conv_b, w_down, ln2_g, ln2_b):
    n_p = x_prompt.shape[0]
    xp, xs = x_prompt, x_sample
    outs_p, outs_s = [], []
    for l in range(DEPTH):
        row = lambda v: v[l].reshape(1, -1)
        heads_per_tile = MXU_TILE // LRU_HEAD_DIM
        wg = (0.5 * jnp.concatenate([_block_diag(lru_wa[l], heads_per_tile),
                                     _block_diag(lru_wx[l], heads_per_tile)], axis=2)).astype(_BF16)
        w_in_scale = jnp.concatenate([jnp.ones((D_MODEL + D_S5,), _F32), jnp.full((2 * D_MODEL,), 0.5, _F32)])
        w_glu_scale = jnp.concatenate([jnp.ones((D_MODEL,), _F32), jnp.full((D_MODEL,), 0.5, _F32)])
        ab_re, ab_im, bb_re, bb_im = _s5_discretise(s5_a_re[l], s5_a_im[l], s5_log_dt[l], s5_b_re[l], s5_b_im[l])
        groups_per_block = S5_IN_BLOCK // S5_GROUP
        to_in = lambda bb: _block_diag(jnp.swapaxes(bb, 1, 2), groups_per_block)
        to_out = lambda cc: _block_diag(jnp.swapaxes(cc, 1, 2), groups_per_block)
        wb = jnp.concatenate([to_in(bb_re), to_in(bb_im)], axis=2).astype(_BF16)
        mixer_w = (
            (w_in[l] * w_in_scale).astype(_BF16), lru_conv_w[l], row(lru_conv_b), wg,
            0.5 * row(lru_ba), 0.5 * row(lru_bx), row(lru_lambda),
            wb, ab_re.reshape(1, D_STATE), ab_im.reshape(1, D_STATE),
            to_out(s5_c_re[l]).astype(_BF16), to_out(s5_c_im[l]).astype(_BF16), row(s5_d),
            (w_glu[l] * w_glu_scale).astype(_BF16), w_out[l].astype(_BF16),
        )
        ffn_w = [row(ln1_g), row(ln1_b), w_up[l], ffn_conv_w[l], row(ffn_conv_b), w_down[l],
                 row(ln2_g), row(ln2_b)]

        (xs, *st_s), (ffn_w[FFN_W_UP], ffn_w[FFN_W_DOWN]) = _layer(
            xs, state_lru_conv[l], state_lru_h[l], state_s5_re[l], state_s5_im[l], state_ffn_conv[l],
            xs.shape[1], mixer_w, ffn_w, cast_ffn_weights=True)
        outs_s.append(st_s)
        zc = jnp.zeros((n_p, LRU_CONV - 1, D_MODEL), _F32)
        zh = jnp.zeros((n_p, D_MODEL), _F32)
        zs = jnp.zeros((n_p, S5_GROUPS, S5_STATE), _F32)
        zf = jnp.zeros((n_p, FFN_CONV - 1, D_FF), _F32)
        xp, *st_p = _layer(xp, zc, zh, zs, zs, zf, PROMPT_TILE_STEPS, mixer_w, ffn_w)
        outs_p.append(st_p)
    stack = lambda outs, j: jnp.stack([o[j] for o in outs])
    return (xp, xs,
            *(stack(outs_p, j) for j in range(5)),
            *(stack(outs_s, j) for j in range(5)))
```

```python
import functools
import math

import jax
import jax.numpy as jnp
from jax.experimental import pallas as pl
from jax.experimental.pallas import tpu as pltpu

D_MODEL = 1024
LRU_HEADS = 16
LRU_HEAD_DIM = D_MODEL // LRU_HEADS
LRU_CONV = 4
LRU_C = 8.0
S5_GROUP = 16
D_S5 = D_MODEL // 2
S5_GROUPS = D_S5 // S5_GROUP
S5_STATE = 64
D_STATE = S5_GROUPS * S5_STATE
D_FF = 3 * D_MODEL
FFN_CONV = 3
DEPTH = 1
ALPHA = (2.0 * DEPTH) ** 0.25
LN_EPS = 1e-5
LOG2_E = math.log2(math.e)
RSQRT_FLOOR = 1e-36

MXU_TILE = 256
SUBLANES = 8
LANES = 128
LANE_BLOCKS = D_MODEL // LANES
GATE_BLOCKS = D_MODEL // MXU_TILE
S5_IN_BLOCK = 128
S5_BLOCKS = D_S5 // S5_IN_BLOCK
S5_STATE_BLOCK = D_STATE // S5_BLOCKS
PROMPT_TILE_STEPS = 64
HEAD_ROWS = 256
FFN_W_UP, FFN_W_DOWN = 2, 5
CAST_CHUNKS_UP, CAST_CHUNKS_DOWN = 32, 16
CAST_SLOTS = 3
VMEM_LIMIT_BYTES = 56 * 1024 * 1024

_BF16 = jnp.bfloat16
_F32 = jnp.float32


def _dot(a, b):
    return jnp.dot(a, b, preferred_element_type=_F32)


def _sigmoid_of_twice(half_z):
    return 0.5 * jnp.tanh(half_z) + 0.5


def _layer_norm(y, g, b):
    mu = jnp.mean(y, axis=-1, keepdims=True)
    yc = y - mu
    var = jnp.mean(yc * yc, axis=-1, keepdims=True)
    return yc * jax.lax.rsqrt(var + LN_EPS) * g + b


def _cast_weight_to_bf16(src_hbm, dst_vmem, out_hbm, stage, in_sems, out_sems, first_out_sem):
    n_slots, chunk = stage.shape[0], stage.shape[1]
    n_chunks = src_hbm.shape[0] // chunk

    def load(c):
        return pltpu.make_async_copy(src_hbm.at[pl.ds(c * chunk, chunk), :], stage.at[c % n_slots],
                                     in_sems.at[c % n_slots])

    out_copies = []
    for c in range(min(n_slots - 1, n_chunks)):
        load(c).start()
    for c in range(n_chunks):
        if c + n_slots - 1 < n_chunks:
            load(c + n_slots - 1).start()
        load(c).wait()
        rows = pl.ds(c * chunk, chunk)
        dst_vmem[rows, :] = stage[c % n_slots].astype(_BF16)
        out_copy = pltpu.make_async_copy(dst_vmem.at[rows, :], out_hbm.at[rows, :], out_sems.at[first_out_sem + c])
        out_copy.start()
        out_copies.append(out_copy)
    return out_copies


def _mixer_kernel(nb, tt, n_tiles, batch_major,
                  x_ref, w_in_ref, cw_ref, cb_ref, wg_ref, ba_ref, bx_ref, lam_ref,
                  wb_ref, are_ref, aim_ref, wcr_ref, wci_ref, d_ref, wglu_ref, wout_ref,
                  conv0_ref, h0_ref, sre0_ref, sim0_ref,
                  y1_ref, conv_out_ref, h_ref, sre_ref, sim_ref,
                  xbuf, abuf, hbuf, bure, buim, gpre, *maybe_xt):
    m = nb * tt
    hist = (LRU_CONV - 1) * nb

    @pl.when(pl.program_id(0) == 0)
    def _():
        xbuf[0:hist, :] = conv0_ref[...]
        h_ref[...] = h0_ref[...]
        sre_ref[...] = sre0_ref[...]
        sim_ref[...] = sim0_ref[...]

    if batch_major:
        xt, x_sems = maybe_xt
        step = pl.program_id(0)
        slot = step % 2

        def tile_copies(tile, into):
            return [pltpu.make_async_copy(x_ref.at[b, pl.ds(tile * tt, tt), :],
                                          xt.at[into, :, b, :],
                                          x_sems.at[into, b]) for b in range(nb)]

        @pl.when(step == 0)
        def _():
            for copy in tile_copies(0, 0):
                copy.start()

        @pl.when(step + 1 < n_tiles)
        def _():
            for copy in tile_copies(step + 1, 1 - slot):
                copy.start()

        for copy in tile_copies(step, slot):
            copy.wait()
        load_x = lambda: xt[slot].reshape(m, D_MODEL)
    else:
        load_x = lambda: x_ref[...]
    xb = load_x().astype(_BF16)

    lam = lam_ref[...]
    c_lam = (0.5 * LRU_C * LOG2_E) * (jnp.minimum(lam, 0.0) - jnp.log1p(jnp.exp(-jnp.abs(lam))))
    gate_cols = D_MODEL + D_S5

    def lru_input_projection(k):
        blk = slice(k * MXU_TILE, (k + 1) * MXU_TILE)
        for r0 in range(0, m, HEAD_ROWS if k == 0 else m):
            rows = slice(r0, r0 + (HEAD_ROWS if k == 0 else m))
            xbuf[hist + rows.start:hist + rows.stop, blk] = _dot(xb[rows, :], w_in_ref[:, blk])

    lru_input_projection(0)
    for k in range(GATE_BLOCKS):
        blk = slice(k * MXU_TILE, (k + 1) * MXU_TILE)
        if k + 1 < GATE_BLOCKS:
            lru_input_projection(k + 1)
        if k == 0:
            u = _dot(xb, w_in_ref[:, D_MODEL:D_MODEL + D_S5])
            ub = u.astype(_BF16)
            for j in range(S5_BLOCKS):
                bu = _dot(ub[:, j * S5_IN_BLOCK:(j + 1) * S5_IN_BLOCK], wb_ref[j])
                cols = slice(j * S5_STATE_BLOCK, (j + 1) * S5_STATE_BLOCK)
                bure[:, cols] = bu[:, 0:S5_STATE_BLOCK]
                buim[:, cols] = bu[:, S5_STATE_BLOCK:]
        xc = cb_ref[:, blk] + xbuf[0:m, blk] * cw_ref[0:1, blk]
        for j in range(1, LRU_CONV):
            xc = xc + xbuf[j * nb:j * nb + m, blk] * cw_ref[j:j + 1, blk]
        new_hist = xbuf[m:m + hist, blk]
        xbuf[0:hist, blk] = new_hist
        conv_out_ref[:, blk] = new_hist
        half_g = _dot(xc.astype(_BF16), wg_ref[k])
        tanh_r = jnp.tanh(half_g[:, 0:MXU_TILE] + ba_ref[:, blk])
        ig = _sigmoid_of_twice(half_g[:, MXU_TILE:] + bx_ref[:, blk])
        a = jnp.exp2(tanh_r * c_lam[:, blk] + c_lam[:, blk])
        abuf[:, blk] = a
        one_minus_a2 = jnp.maximum((1.0 - a) * (1.0 + a), 0.0)
        mult = one_minus_a2 * jax.lax.rsqrt(jnp.maximum(one_minus_a2, RSQRT_FLOOR))
        hbuf[:, blk] = mult * (ig * xc)
        for half in range(2):
            gblk = slice((2 * k + half) * MXU_TILE, (2 * k + half + 1) * MXU_TILE)
            gpre[:, gblk] = _dot(xb, w_in_ref[:, gate_cols + gblk.start:gate_cols + gblk.stop])

    h = h_ref[...]
    for t in range(tt):
        rows = slice(t * nb, (t + 1) * nb)
        h = abuf[rows, :] * h + hbuf[rows, :]
        hbuf[rows, :] = h
    h_ref[...] = h

    ys = []
    for k in range(S5_BLOCKS):
        cols = slice(k * S5_STATE_BLOCK, (k + 1) * S5_STATE_BLOCK)
        ar = jnp.broadcast_to(are_ref[:, cols], (nb, S5_STATE_BLOCK))
        ai = jnp.broadcast_to(aim_ref[:, cols], (nb, S5_STATE_BLOCK))
        hr = sre_ref[:, cols]
        hi = sim_ref[:, cols]
        for t in range(tt):
            rows = slice(t * nb, (t + 1) * nb)
            hr_new = ar * hr - ai * hi + bure[rows, cols]
            hi_new = ar * hi + ai * hr + buim[rows, cols]
            hr, hi = hr_new, hi_new
            bure[rows, cols] = hr
            buim[rows, cols] = hi
        sre_ref[:, cols] = hr
        sim_ref[:, cols] = hi
        ys.append(_dot(bure[:, cols].astype(_BF16), wcr_ref[k])
                  - _dot(buim[:, cols].astype(_BF16), wci_ref[k]))
    y = jnp.concatenate(ys, axis=1) + d_ref[...] * u
    z = jax.nn.gelu(y).astype(_BF16)
    glu = _dot(z, wglu_ref[...])
    s5_out = glu[:, 0:D_MODEL] * _sigmoid_of_twice(glu[:, D_MODEL:])

    g_lru = _sigmoid_of_twice(gpre[:, 0:D_MODEL])
    g_s5 = _sigmoid_of_twice(gpre[:, D_MODEL:])
    merged = (g_lru * hbuf[...] + g_s5 * s5_out).astype(_BF16)
    mix = _dot(merged, wout_ref[...])
    y1_ref[...] = ALPHA * load_x() + mix


def _ffn_kernel(nb, tt, n_tiles, batch_major, cast_weights,
                y1_ref, g1_ref, b1_ref, wup_ref, cw_ref, cb_ref, wdown_ref, g2_ref, b2_ref, conv0_ref,
                out_ref, conv_out_ref, *rest):
    m = nb * tt
    hist = (FFN_CONV - 1) * nb
    step = pl.program_id(0)
    weight_copies = []
    if cast_weights:
        assert n_tiles == 1
        (wup_out, wdown_out, abuf, wup_b, wdown_b, up_stage, down_stage, in_sems, out_sems, *maybe_yt) = rest
        n_up = wup_ref.shape[0] // up_stage.shape[1]
        weight_copies += _cast_weight_to_bf16(wup_ref, wup_b, wup_out, up_stage, in_sems, out_sems, 0)
        weight_copies += _cast_weight_to_bf16(wdown_ref, wdown_b, wdown_out, down_stage, in_sems, out_sems, n_up)
        wup_ref, wdown_ref = wup_b, wdown_b
    else:
        (abuf, *maybe_yt) = rest

    @pl.when(step == 0)
    def _():
        abuf[0:hist, :] = conv0_ref[...]

    if batch_major:
        yt, y_sems = maybe_yt
        slot = step % 2

        def tile_copies(tile, staged):
            return [pltpu.make_async_copy(yt.at[staged, :, b, :],
                                          out_ref.at[b, pl.ds(tile * tt, tt), :],
                                          y_sems.at[staged, b]) for b in range(nb)]

        @pl.when(step >= 2)
        def _():
            for copy in tile_copies(step - 2, slot):
                copy.wait()

    x = _layer_norm(y1_ref[...], g1_ref[...], b1_ref[...])
    xb = x.astype(_BF16)
    for r0 in range(0, m, HEAD_ROWS):
        abuf[hist + r0:hist + r0 + HEAD_ROWS, 0:D_MODEL] = _dot(xb[r0:r0 + HEAD_ROWS, :], wup_ref[:, 0:D_MODEL])
    abuf[hist:hist + m, D_MODEL:] = _dot(xb, wup_ref[:, D_MODEL:D_FF])
    f = None
    for c in range(D_FF // D_MODEL):
        cols = slice(c * D_MODEL, (c + 1) * D_MODEL)
        ac = cb_ref[:, cols] + abuf[0:m, cols] * cw_ref[0:1, cols]
        for k in range(1, FFN_CONV):
            ac = ac + abuf[k * nb:k * nb + m, cols] * cw_ref[k:k + 1, cols]
        gate = _dot(xb, wup_ref[:, D_FF + c * D_MODEL:D_FF + (c + 1) * D_MODEL])
        hmid = (jax.nn.gelu(ac) * gate).astype(_BF16)
        part = _dot(hmid, wdown_ref[cols, :])
        f = part if f is None else f + part
    new_hist = abuf[m:m + hist, :]
    abuf[0:hist, :] = new_hist
    conv_out_ref[...] = new_hist
    y = _layer_norm(ALPHA * x + f, g2_ref[...], b2_ref[...])
    if batch_major:
        yt[slot] = y.reshape(tt, nb, D_MODEL)
        for copy in tile_copies(step, slot):
            copy.start()

        @pl.when(step == n_tiles - 1)
        def _():
            if n_tiles > 1:
                for copy in tile_copies(step - 1, 1 - slot):
                    copy.wait()
            for copy in tile_copies(step, slot):
                copy.wait()
    else:
        out_ref[...] = y
    for copy in weight_copies:
        copy.wait()


def _resident(shape):
    zeros = (0,) * len(shape)
    return pl.BlockSpec(shape, lambda i: zeros, pipeline_mode=pl.Buffered(1))


def _run_mixer(x, nb, tt, weights, states, batch_major):
    m = nb * tt
    hist = (LRU_CONV - 1) * nb
    tile = pl.BlockSpec((m, D_MODEL), lambda i: (i, 0))
    if batch_major:
        n_rows = nb * x.shape[1]
        x_spec = pl.BlockSpec(memory_space=pl.ANY)
    else:
        n_rows = x.shape[0]
        x_spec = tile
    operands = (x,) + tuple(weights) + tuple(states)
    in_specs = [x_spec] + [_resident(a.shape) for a in operands[1:]]
    out_shape = (
        jax.ShapeDtypeStruct((n_rows, D_MODEL), _F32),
        jax.ShapeDtypeStruct((hist, D_MODEL), _F32),
        jax.ShapeDtypeStruct((nb, D_MODEL), _F32),
        jax.ShapeDtypeStruct((nb, D_STATE), _F32),
        jax.ShapeDtypeStruct((nb, D_STATE), _F32),
    )
    out_specs = [tile] + [pl.BlockSpec(s.shape, lambda i: (0, 0)) for s in out_shape[1:]]
    scratch_shapes = [
        pltpu.VMEM((hist + m, D_MODEL), _F32),
        pltpu.VMEM((m, D_MODEL), _F32),
        pltpu.VMEM((m, D_MODEL), _F32),
        pltpu.VMEM((m, D_STATE), _F32),
        pltpu.VMEM((m, D_STATE), _F32),
        pltpu.VMEM((m, 2 * D_MODEL), _F32),
    ]
    if batch_major:
        scratch_shapes += [pltpu.VMEM((2, tt, nb, D_MODEL), _F32), pltpu.SemaphoreType.DMA((2, nb))]
    return pl.pallas_call(
        functools.partial(_mixer_kernel, nb, tt, n_rows // m, batch_major),
        out_shape=out_shape,
        grid=(n_rows // m,),
        in_specs=in_specs,
        out_specs=out_specs,
        scratch_shapes=scratch_shapes,
        compiler_params=pltpu.CompilerParams(
            dimension_semantics=("arbitrary",), vmem_limit_bytes=VMEM_LIMIT_BYTES),
        name=f"mixer_nb{nb}",
    )(*operands)


def _run_ffn(x_rows, nb, tt, weights, conv0, batch_major, cast_weights):
    n_rows = x_rows.shape[0]
    m = nb * tt
    hist = (FFN_CONV - 1) * nb
    tile = pl.BlockSpec((m, D_MODEL), lambda i: (i, 0))
    in_hbm = pl.BlockSpec(memory_space=pl.ANY)
    operands = (x_rows,) + tuple(weights) + (conv0,)
    in_specs = [tile] + [_resident(a.shape) for a in operands[1:]]
    if batch_major:
        y_shape = jax.ShapeDtypeStruct((nb, n_rows // nb, D_MODEL), _F32)
        y_spec = in_hbm
    else:
        y_shape = jax.ShapeDtypeStruct((n_rows, D_MODEL), _F32)
        y_spec = tile
    out_shape = [y_shape, jax.ShapeDtypeStruct((hist, D_FF), _F32)]
    out_specs = [y_spec, pl.BlockSpec((hist, D_FF), lambda i: (0, 0))]
    scratch_shapes = [pltpu.VMEM((hist + m, D_FF), _F32)]
    if cast_weights:
        w_up, w_down = weights[FFN_W_UP], weights[FFN_W_DOWN]
        in_specs[1 + FFN_W_UP] = in_hbm
        in_specs[1 + FFN_W_DOWN] = in_hbm
        out_shape += [jax.ShapeDtypeStruct(w_up.shape, _BF16), jax.ShapeDtypeStruct(w_down.shape, _BF16)]
        out_specs += [in_hbm, in_hbm]
        up_chunk, down_chunk = w_up.shape[0] // CAST_CHUNKS_UP, w_down.shape[0] // CAST_CHUNKS_DOWN
        scratch_shapes += [
            pltpu.VMEM(w_up.shape, _BF16), pltpu.VMEM(w_down.shape, _BF16),
            pltpu.VMEM((CAST_SLOTS, up_chunk, w_up.shape[1]), _F32),
            pltpu.VMEM((CAST_SLOTS, down_chunk, w_down.shape[1]), _F32),
            pltpu.SemaphoreType.DMA((CAST_SLOTS,)), pltpu.SemaphoreType.DMA((CAST_CHUNKS_UP + CAST_CHUNKS_DOWN,)),
        ]
    if batch_major:
        scratch_shapes += [pltpu.VMEM((2, tt, nb, D_MODEL), _F32), pltpu.SemaphoreType.DMA((2, nb))]
    return pl.pallas_call(
        functools.partial(_ffn_kernel, nb, tt, n_rows // m, batch_major, cast_weights),
        out_shape=tuple(out_shape),
        grid=(n_rows // m,),
        in_specs=in_specs,
        out_specs=out_specs,
        scratch_shapes=scratch_shapes,
        compiler_params=pltpu.CompilerParams(
            dimension_semantics=("arbitrary",), vmem_limit_bytes=VMEM_LIMIT_BYTES),
        name=f"ffn_nb{nb}",
    )(*operands)


def _block_diag(blocks, n_per_tile):
    n, r, c = blocks.shape
    tiles = n // n_per_tile
    eye = jnp.eye(n_per_tile, dtype=blocks.dtype)
    b = blocks.reshape(tiles, n_per_tile, r, c)
    return jnp.einsum("tgrc,gh->tgrhc", b, eye).reshape(tiles, n_per_tile * r, n_per_tile * c)


def _s5_discretise(a_re, a_im, log_dt, b_re, b_im):
    dt = jnp.exp(log_dt)[:, None]
    mag = jnp.exp(a_re * dt)
    ab_re = mag * jnp.cos(a_im * dt)
    ab_im = mag * jnp.sin(a_im * dt)
    nr = ab_re - 1.0
    ni = ab_im
    den = a_re * a_re + a_im * a_im
    coef_re = (nr * a_re + ni * a_im) / den
    coef_im = (ni * a_re - nr * a_im) / den
    bb_re = coef_re[..., None] * b_re - coef_im[..., None] * b_im
    bb_im = coef_re[..., None] * b_im + coef_im[..., None] * b_re
    return ab_re, ab_im, bb_re, bb_im


def _time_major(x):
    b, l, c = x.shape
    return jnp.transpose(x, (1, 0, 2)).reshape(l * b, c)


def _batch_major(rows, b):
    n, c = rows.shape
    return jnp.transpose(rows.reshape(n // b, b, c), (1, 0, 2))


def _layer(x, lru_conv0, lru_h0, s5_re0, s5_im0, ffn_conv0, tt, mixer_w, ffn_w, cast_ffn_weights=False):
    nb = x.shape[0]
    batch_major = tt % SUBLANES == 0
    states = (_time_major(lru_conv0), lru_h0,
              s5_re0.reshape(nb, D_STATE), s5_im0.reshape(nb, D_STATE))
    x1, conv_out, h_out, sre, sim = _run_mixer(
        x if batch_major else _time_major(x), nb, tt, mixer_w, states, batch_major)
    y, fconv_out, *cast = _run_ffn(x1, nb, tt, ffn_w, _time_major(ffn_conv0), batch_major, cast_ffn_weights)
    outs = (y if batch_major else _batch_major(y, nb), _batch_major(conv_out, nb), h_out,
            sre.reshape(nb, S5_GROUPS, S5_STATE), sim.reshape(nb, S5_GROUPS, S5_STATE),
            _batch_major(fconv_out, nb))
    return (outs, tuple(cast)) if cast_ffn_weights else outs


def kernel(x_prompt, x_sample, state_lru_conv, state_lru_h, state_s5_re, state_s5_im, state_ffn_conv, w_in, lru_conv_w, lru_conv_b, lru_wa, lru_ba, lru_wx, lru_bx, lru_lambda, s5_a_re, s5_a_im, s5_log_dt, s5_b_re, s5_b_im, s5_c_re, s5_c_im, s5_d, w_glu, w_out, ln1_g, ln1_b, w_up, ffn_conv_w, ffn_conv_b, w_down, ln2_g, ln2_b):
    n_p = x_prompt.shape[0]
    xp, xs = x_prompt, x_sample
    outs_p, outs_s = [], []
    for l in range(DEPTH):
        row = lambda v: v[l].reshape(1, -1)
        heads_per_tile = MXU_TILE // LRU_HEAD_DIM
        wg = (0.5 * jnp.concatenate([_block_diag(lru_wa[l], heads_per_tile),
                                     _block_diag(lru_wx[l], heads_per_tile)], axis=2)).astype(_BF16)
        w_in_scale = jnp.concatenate([jnp.ones((D_MODEL + D_S5,), _F32), jnp.full((2 * D_MODEL,), 0.5, _F32)])
        w_glu_scale = jnp.concatenate([jnp.ones((D_MODEL,), _F32), jnp.full((D_MODEL,), 0.5, _F32)])
        ab_re, ab_im, bb_re, bb_im = _s5_discretise(s5_a_re[l], s5_a_im[l], s5_log_dt[l], s5_b_re[l], s5_b_im[l])
        groups_per_block = S5_IN_BLOCK // S5_GROUP
        to_in = lambda bb: _block_diag(jnp.swapaxes(bb, 1, 2), groups_per_block)
        to_out = lambda cc: _block_diag(jnp.swapaxes(cc, 1, 2), groups_per_block)
        wb = jnp.concatenate([to_in(bb_re), to_in(bb_im)], axis=2).astype(_BF16)
        mixer_w = (
            (w_in[l] * w_in_scale).astype(_BF16), lru_conv_w[l], row(lru_conv_b), wg,
            0.5 * row(lru_ba), 0.5 * row(lru_bx), row(lru_lambda),
            wb, ab_re.reshape(1, D_STATE), ab_im.reshape(1, D_STATE),
            to_out(s5_c_re[l]).astype(_BF16), to_out(s5_c_im[l]).astype(_BF16), row(s5_d),
            (w_glu[l] * w_glu_scale).astype(_BF16), w_out[l].astype(_BF16),
        )
        ffn_w = [row(ln1_g), row(ln1_b), w_up[l], ffn_conv_w[l], row(ffn_conv_b), w_down[l],
                 row(ln2_g), row(ln2_b)]

        (xs, *st_s), (ffn_w[FFN_W_UP], ffn_w[FFN_W_DOWN]) = _layer(
            xs, state_lru_conv[l], state_lru_h[l], state_s5_re[l], state_s5_im[l], state_ffn_conv[l],
            xs.shape[1], mixer_w, ffn_w, cast_ffn_weights=True)
        outs_s.append(st_s)
        zc = jnp.zeros((n_p, LRU_CONV - 1, D_MODEL), _F32)
        zh = jnp.zeros((n_p, D_MODEL), _F32)
        zs = jnp.zeros((n_p, S5_GROUPS, S5_STATE), _F32)
        zf = jnp.zeros((n_p, FFN_CONV - 1, D_FF), _F32)
        xp, *st_p = _layer(xp, zc, zh, zs, zs, zf, PROMPT_TILE_STEPS, mixer_w, ffn_w)
        outs_p.append(st_p)
    stack = lambda outs, j: jnp.stack([o[j] for o in outs])
    return (xp, xs,
            *(stack(outs_p, j) for j in range(5)),
            *(stack(outs_s, j) for j in range(5)))
```

```python
import functools
import math

import jax
import jax.numpy as jnp
from jax.experimental import pallas as pl
from jax.experimental.pallas import tpu as pltpu

D_MODEL = 1024
LRU_HEADS = 16
LRU_HEAD_DIM = D_MODEL // LRU_HEADS
LRU_CONV = 4
LRU_C = 8.0
S5_GROUP = 16
D_S5 = D_MODEL // 2
S5_GROUPS = D_S5 // S5_GROUP
S5_STATE = 64
D_STATE = S5_GROUPS * S5_STATE
D_FF = 3 * D_MODEL
FFN_CONV = 3
DEPTH = 1
ALPHA = (2.0 * DEPTH) ** 0.25
LN_EPS = 1e-5
LOG2_E = math.log2(math.e)
RSQRT_FLOOR = 1e-36

MXU_TILE = 256
SUBLANES = 8
LANES = 128
LANE_BLOCKS = D_MODEL // LANES
GATE_BLOCKS = D_MODEL // MXU_TILE
S5_IN_BLOCK = 128
S5_BLOCKS = D_S5 // S5_IN_BLOCK
S5_STATE_BLOCK = D_STATE // S5_BLOCKS
PROMPT_TILE_STEPS = 64
HEAD_ROWS = 256
VMEM_LIMIT_BYTES = 56 * 1024 * 1024

_BF16 = jnp.bfloat16
_F32 = jnp.float32


def _dot(a, b):
    return jnp.dot(a, b, preferred_element_type=_F32)


def _sigmoid_of_twice(half_z):
    return 0.5 * jnp.tanh(half_z) + 0.5


def _layer_norm(y, g, b):
    mu = jnp.mean(y, axis=-1, keepdims=True)
    yc = y - mu
    var = jnp.mean(yc * yc, axis=-1, keepdims=True)
    return yc * jax.lax.rsqrt(var + LN_EPS) * g + b


def _mixer_kernel(nb, tt, n_tiles, batch_major,
                  x_ref, w_in_ref, cw_ref, cb_ref, wg_ref, ba_ref, bx_ref, lam_ref,
                  wb_ref, are_ref, aim_ref, wcr_ref, wci_ref, d_ref, wglu_ref, wout_ref,
                  conv0_ref, h0_ref, sre0_ref, sim0_ref,
                  y1_ref, conv_out_ref, h_ref, sre_ref, sim_ref,
                  xbuf, abuf, hbuf, bure, buim, gpre, *maybe_xt):
    m = nb * tt
    hist = (LRU_CONV - 1) * nb

    @pl.when(pl.program_id(0) == 0)
    def _():
        xbuf[0:hist, :] = conv0_ref[...]
        h_ref[...] = h0_ref[...]
        sre_ref[...] = sre0_ref[...]
        sim_ref[...] = sim0_ref[...]

    if batch_major:
        xt, x_sems = maybe_xt
        step = pl.program_id(0)
        slot = step % 2

        def tile_copies(tile, into):
            return [pltpu.make_async_copy(x_ref.at[b, pl.ds(tile * tt, tt), :],
                                          xt.at[into, :, b, :],
                                          x_sems.at[into, b]) for b in range(nb)]

        @pl.when(step == 0)
        def _():
            for copy in tile_copies(0, 0):
                copy.start()

        @pl.when(step + 1 < n_tiles)
        def _():
            for copy in tile_copies(step + 1, 1 - slot):
                copy.start()

        for copy in tile_copies(step, slot):
            copy.wait()
        load_x = lambda: xt[slot].reshape(m, D_MODEL)
    else:
        load_x = lambda: jnp.concatenate([x_ref[:, t, :] for t in range(tt)], axis=0)
    xb = load_x().astype(_BF16)

    lam = lam_ref[...]
    c_lam = (0.5 * LRU_C * LOG2_E) * (jnp.minimum(lam, 0.0) - jnp.log1p(jnp.exp(-jnp.abs(lam))))
    gate_cols = D_MODEL + D_S5

    def lru_input_projection(k):
        blk = slice(k * MXU_TILE, (k + 1) * MXU_TILE)
        for r0 in range(0, m, HEAD_ROWS if k == 0 else m):
            rows = slice(r0, r0 + (HEAD_ROWS if k == 0 else m))
            xbuf[hist + rows.start:hist + rows.stop, blk] = _dot(xb[rows, :], w_in_ref[:, blk])

    lru_input_projection(0)
    for k in range(GATE_BLOCKS):
        blk = slice(k * MXU_TILE, (k + 1) * MXU_TILE)
        if k + 1 < GATE_BLOCKS:
            lru_input_projection(k + 1)
        if k == 0:
            u = _dot(xb, w_in_ref[:, D_MODEL:D_MODEL + D_S5])
            ub = u.astype(_BF16)
            for j in range(S5_BLOCKS):
                bu = _dot(ub[:, j * S5_IN_BLOCK:(j + 1) * S5_IN_BLOCK], wb_ref[j])
                cols = slice(j * S5_STATE_BLOCK, (j + 1) * S5_STATE_BLOCK)
                bure[:, cols] = bu[:, 0:S5_STATE_BLOCK]
                buim[:, cols] = bu[:, S5_STATE_BLOCK:]
        xc = cb_ref[:, blk] + xbuf[0:m, blk] * cw_ref[0:1, blk]
        for j in range(1, LRU_CONV):
            xc = xc + xbuf[j * nb:j * nb + m, blk] * cw_ref[j:j + 1, blk]
        new_hist = xbuf[m:m + hist, blk]
        xbuf[0:hist, blk] = new_hist
        conv_out_ref[:, blk] = new_hist
        half_g = _dot(xc.astype(_BF16), wg_ref[k])
        tanh_r = jnp.tanh(half_g[:, 0:MXU_TILE] + ba_ref[:, blk])
        ig = _sigmoid_of_twice(half_g[:, MXU_TILE:] + bx_ref[:, blk])
        a = jnp.exp2(tanh_r * c_lam[:, blk] + c_lam[:, blk])
        abuf[:, blk] = a
        one_minus_a2 = jnp.maximum((1.0 - a) * (1.0 + a), 0.0)
        mult = one_minus_a2 * jax.lax.rsqrt(jnp.maximum(one_minus_a2, RSQRT_FLOOR))
        hbuf[:, blk] = mult * (ig * xc)
        for half in range(2):
            gblk = slice((2 * k + half) * MXU_TILE, (2 * k + half + 1) * MXU_TILE)
            gpre[:, gblk] = _dot(xb, w_in_ref[:, gate_cols + gblk.start:gate_cols + gblk.stop])

    h = h_ref[...]
    for t in range(tt):
        rows = slice(t * nb, (t + 1) * nb)
        h = abuf[rows, :] * h + hbuf[rows, :]
        hbuf[rows, :] = h
    h_ref[...] = h

    ys = []
    for k in range(S5_BLOCKS):
        cols = slice(k * S5_STATE_BLOCK, (k + 1) * S5_STATE_BLOCK)
        ar = jnp.broadcast_to(are_ref[:, cols], (nb, S5_STATE_BLOCK))
        ai = jnp.broadcast_to(aim_ref[:, cols], (nb, S5_STATE_BLOCK))
        hr = sre_ref[:, cols]
        hi = sim_ref[:, cols]
        for t in range(tt):
            rows = slice(t * nb, (t + 1) * nb)
            hr_new = ar * hr - ai * hi + bure[rows, cols]
            hi_new = ar * hi + ai * hr + buim[rows, cols]
            hr, hi = hr_new, hi_new
            bure[rows, cols] = hr
            buim[rows, cols] = hi
        sre_ref[:, cols] = hr
        sim_ref[:, cols] = hi
        ys.append(_dot(bure[:, cols].astype(_BF16), wcr_ref[k])
                  - _dot(buim[:, cols].astype(_BF16), wci_ref[k]))
    y = jnp.concatenate(ys, axis=1) + d_ref[...] * u
    z = jax.nn.gelu(y).astype(_BF16)
    glu = _dot(z, wglu_ref[...])
    s5_out = glu[:, 0:D_MODEL] * _sigmoid_of_twice(glu[:, D_MODEL:])

    g_lru = _sigmoid_of_twice(gpre[:, 0:D_MODEL])
    g_s5 = _sigmoid_of_twice(gpre[:, D_MODEL:])
    merged = (g_lru * hbuf[...] + g_s5 * s5_out).astype(_BF16)
    mix = _dot(merged, wout_ref[...])
    y1_ref[...] = ALPHA * load_x() + mix


def _ffn_kernel(nb, tt, n_tiles, batch_major,
                y1_ref, g1_ref, b1_ref, wup_ref, cw_ref, cb_ref, wdown_ref, g2_ref, b2_ref, conv0_ref,
                out_ref, conv_out_ref,
                abuf, *maybe_yt):
    m = nb * tt
    hist = (FFN_CONV - 1) * nb
    step = pl.program_id(0)

    @pl.when(step == 0)
    def _():
        abuf[0:hist, :] = conv0_ref[...]

    if batch_major:
        yt, y_sems = maybe_yt
        slot = step % 2

        def tile_copies(tile, staged):
            return [pltpu.make_async_copy(yt.at[staged, :, b, :],
                                          out_ref.at[b, pl.ds(tile * tt, tt), :],
                                          y_sems.at[staged, b]) for b in range(nb)]

        @pl.when(step >= 2)
        def _():
            for copy in tile_copies(step - 2, slot):
                copy.wait()

    x = _layer_norm(y1_ref[...], g1_ref[...], b1_ref[...])
    xb = x.astype(_BF16)
    for r0 in range(0, m, HEAD_ROWS):
        abuf[hist + r0:hist + r0 + HEAD_ROWS, 0:D_MODEL] = _dot(xb[r0:r0 + HEAD_ROWS, :], wup_ref[:, 0:D_MODEL])
    abuf[hist:hist + m, D_MODEL:] = _dot(xb, wup_ref[:, D_MODEL:D_FF])
    f = None
    for c in range(D_FF // D_MODEL):
        cols = slice(c * D_MODEL, (c + 1) * D_MODEL)
        ac = cb_ref[:, cols] + abuf[0:m, cols] * cw_ref[0:1, cols]
        for k in range(1, FFN_CONV):
            ac = ac + abuf[k * nb:k * nb + m, cols] * cw_ref[k:k + 1, cols]
        gate = _dot(xb, wup_ref[:, D_FF + c * D_MODEL:D_FF + (c + 1) * D_MODEL])
        hmid = (jax.nn.gelu(ac) * gate).astype(_BF16)
        part = _dot(hmid, wdown_ref[cols, :])
        f = part if f is None else f + part
    new_hist = abuf[m:m + hist, :]
    abuf[0:hist, :] = new_hist
    conv_out_ref[...] = new_hist
    y = _layer_norm(ALPHA * x + f, g2_ref[...], b2_ref[...])
    if batch_major:
        yt[slot] = y.reshape(tt, nb, D_MODEL)
        for copy in tile_copies(step, slot):
            copy.start()

        @pl.when(step == n_tiles - 1)
        def _():
            if n_tiles > 1:
                for copy in tile_copies(step - 1, 1 - slot):
                    copy.wait()
            for copy in tile_copies(step, slot):
                copy.wait()
    else:
        for t in range(tt):
            out_ref[:, t, :] = y[t * nb:(t + 1) * nb, :]


def _resident(shape):
    zeros = (0,) * len(shape)
    return pl.BlockSpec(shape, lambda i: zeros, pipeline_mode=pl.Buffered(1))


def _run_mixer(x, nb, tt, weights, states, batch_major):
    m = nb * tt
    hist = (LRU_CONV - 1) * nb
    tile = pl.BlockSpec((m, D_MODEL), lambda i: (i, 0))
    if batch_major:
        n_rows = nb * x.shape[1]
        x_spec = pl.BlockSpec(memory_space=pl.ANY)
    else:
        n_rows = nb * x.shape[1]
        x_spec = pl.BlockSpec((nb, tt, D_MODEL), lambda i: (0, i, 0))
    operands = (x,) + tuple(weights) + tuple(states)
    in_specs = [x_spec] + [_resident(a.shape) for a in operands[1:]]
    out_shape = (
        jax.ShapeDtypeStruct((n_rows, D_MODEL), _F32),
        jax.ShapeDtypeStruct((hist, D_MODEL), _F32),
        jax.ShapeDtypeStruct((nb, D_MODEL), _F32),
        jax.ShapeDtypeStruct((nb, D_STATE), _F32),
        jax.ShapeDtypeStruct((nb, D_STATE), _F32),
    )
    out_specs = [tile] + [pl.BlockSpec(s.shape, lambda i: (0, 0)) for s in out_shape[1:]]
    scratch_shapes = [
        pltpu.VMEM((hist + m, D_MODEL), _F32),
        pltpu.VMEM((m, D_MODEL), _F32),
        pltpu.VMEM((m, D_MODEL), _F32),
        pltpu.VMEM((m, D_STATE), _F32),
        pltpu.VMEM((m, D_STATE), _F32),
        pltpu.VMEM((m, 2 * D_MODEL), _F32),
    ]
    if batch_major:
        scratch_shapes += [pltpu.VMEM((2, tt, nb, D_MODEL), _F32), pltpu.SemaphoreType.DMA((2, nb))]
    return pl.pallas_call(
        functools.partial(_mixer_kernel, nb, tt, n_rows // m, batch_major),
        out_shape=out_shape,
        grid=(n_rows // m,),
        in_specs=in_specs,
        out_specs=out_specs,
        scratch_shapes=scratch_shapes,
        compiler_params=pltpu.CompilerParams(
            dimension_semantics=("arbitrary",), vmem_limit_bytes=VMEM_LIMIT_BYTES),
        name=f"mixer_nb{nb}",
    )(*operands)


def _run_ffn(x_rows, nb, tt, weights, conv0, batch_major):
    n_rows = x_rows.shape[0]
    m = nb * tt
    hist = (FFN_CONV - 1) * nb
    tile = pl.BlockSpec((m, D_MODEL), lambda i: (i, 0))
    operands = (x_rows,) + tuple(weights) + (conv0,)
    in_specs = [tile] + [_resident(a.shape) for a in operands[1:]]
    y_shape = jax.ShapeDtypeStruct((nb, n_rows // nb, D_MODEL), _F32)
    if batch_major:
        y_spec = pl.BlockSpec(memory_space=pl.ANY)
    else:
        y_spec = pl.BlockSpec((nb, tt, D_MODEL), lambda i: (0, i, 0))
    out_shape = (y_shape, jax.ShapeDtypeStruct((hist, D_FF), _F32))
    out_specs = [y_spec, pl.BlockSpec((hist, D_FF), lambda i: (0, 0))]
    scratch_shapes = [pltpu.VMEM((hist + m, D_FF), _F32)]
    if batch_major:
        scratch_shapes += [pltpu.VMEM((2, tt, nb, D_MODEL), _F32), pltpu.SemaphoreType.DMA((2, nb))]
    return pl.pallas_call(
        functools.partial(_ffn_kernel, nb, tt, n_rows // m, batch_major),
        out_shape=out_shape,
        grid=(n_rows // m,),
        in_specs=in_specs,
        out_specs=out_specs,
        scratch_shapes=scratch_shapes,
        compiler_params=pltpu.CompilerParams(
            dimension_semantics=("arbitrary",), vmem_limit_bytes=VMEM_LIMIT_BYTES),
        name=f"ffn_nb{nb}",
    )(*operands)


def _block_diag(blocks, n_per_tile):
    n, r, c = blocks.shape
    tiles = n // n_per_tile
    eye = jnp.eye(n_per_tile, dtype=blocks.dtype)
    b = blocks.reshape(tiles, n_per_tile, r, c)
    return jnp.einsum("tgrc,gh->tgrhc", b, eye).reshape(tiles, n_per_tile * r, n_per_tile * c)


def _s5_discretise(a_re, a_im, log_dt, b_re, b_im):
    dt = jnp.exp(log_dt)[:, None]
    mag = jnp.exp(a_re * dt)
    ab_re = mag * jnp.cos(a_im * dt)
    ab_im = mag * jnp.sin(a_im * dt)
    nr = ab_re - 1.0
    ni = ab_im
    den = a_re * a_re + a_im * a_im
    coef_re = (nr * a_re + ni * a_im) / den
    coef_im = (ni * a_re - nr * a_im) / den
    bb_re = coef_re[..., None] * b_re - coef_im[..., None] * b_im
    bb_im = coef_re[..., None] * b_im + coef_im[..., None] * b_re
    return ab_re, ab_im, bb_re, bb_im


def _time_major(x):
    b, l, c = x.shape
    return jnp.transpose(x, (1, 0, 2)).reshape(l * b, c)


def _batch_major(rows, b):
    n, c = rows.shape
    return jnp.transpose(rows.reshape(n // b, b, c), (1, 0, 2))


def _layer(x, lru_conv0, lru_h0, s5_re0, s5_im0, ffn_conv0, tt, mixer_w, ffn_w):
    nb = x.shape[0]
    batch_major = tt % SUBLANES == 0
    states = (_time_major(lru_conv0), lru_h0,
              s5_re0.reshape(nb, D_STATE), s5_im0.reshape(nb, D_STATE))
    x1, conv_out, h_out, sre, sim = _run_mixer(
        x, nb, tt, mixer_w, states, batch_major)
    y, fconv_out = _run_ffn(x1, nb, tt, ffn_w, _time_major(ffn_conv0), batch_major)
    return (y, _batch_major(conv_out, nb), h_out,
            sre.reshape(nb, S5_GROUPS, S5_STATE), sim.reshape(nb, S5_GROUPS, S5_STATE),
            _batch_major(fconv_out, nb))


def kernel(x_prompt, x_sample, state_lru_conv, state_lru_h, state_s5_re, state_s5_im, state_ffn_conv, w_in, lru_conv_w, lru_conv_b, lru_wa, lru_ba, lru_wx, lru_bx, lru_lambda, s5_a_re, s5_a_im, s5_log_dt, s5_b_re, s5_b_im, s5_c_re, s5_c_im, s5_d, w_glu, w_out, ln1_g, ln1_b, w_up, ffn_conv_w, ffn_conv_b, w_down, ln2_g, ln2_b):
    n_p = x_prompt.shape[0]
    xp, xs = x_prompt, x_sample
    outs_p, outs_s = [], []
    for l in range(DEPTH):
        row = lambda v: v[l].reshape(1, -1)
        heads_per_tile = MXU_TILE // LRU_HEAD_DIM
        wg = (0.5 * jnp.concatenate([_block_diag(lru_wa[l], heads_per_tile),
                                     _block_diag(lru_wx[l], heads_per_tile)], axis=2)).astype(_BF16)
        w_in_scale = jnp.concatenate([jnp.ones((D_MODEL + D_S5,), _F32), jnp.full((2 * D_MODEL,), 0.5, _F32)])
        w_glu_scale = jnp.concatenate([jnp.ones((D_MODEL,), _F32), jnp.full((D_MODEL,), 0.5, _F32)])
        ab_re, ab_im, bb_re, bb_im = _s5_discretise(s5_a_re[l], s5_a_im[l], s5_log_dt[l], s5_b_re[l], s5_b_im[l])
        groups_per_block = S5_IN_BLOCK // S5_GROUP
        to_in = lambda bb: _block_diag(jnp.swapaxes(bb, 1, 2), groups_per_block)
        to_out = lambda cc: _block_diag(jnp.swapaxes(cc, 1, 2), groups_per_block)
        wb = jnp.concatenate([to_in(bb_re), to_in(bb_im)], axis=2).astype(_BF16)
        mixer_w = (
            (w_in[l] * w_in_scale).astype(_BF16), lru_conv_w[l], row(lru_conv_b), wg,
            0.5 * row(lru_ba), 0.5 * row(lru_bx), row(lru_lambda),
            wb, ab_re.reshape(1, D_STATE), ab_im.reshape(1, D_STATE),
            to_out(s5_c_re[l]).astype(_BF16), to_out(s5_c_im[l]).astype(_BF16), row(s5_d),
            (w_glu[l] * w_glu_scale).astype(_BF16), w_out[l].astype(_BF16),
        )
        ffn_w = (row(ln1_g), row(ln1_b), w_up[l].astype(_BF16), ffn_conv_w[l], row(ffn_conv_b), w_down[l].astype(_BF16),
                 row(ln2_g), row(ln2_b))

        zc = jnp.zeros((n_p, LRU_CONV - 1, D_MODEL), _F32)
        zh = jnp.zeros((n_p, D_MODEL), _F32)
        zs = jnp.zeros((n_p, S5_GROUPS, S5_STATE), _F32)
        zf = jnp.zeros((n_p, FFN_CONV - 1, D_FF), _F32)
        xp, *st_p = _layer(xp, zc, zh, zs, zs, zf, PROMPT_TILE_STEPS, mixer_w, ffn_w)
        outs_p.append(st_p)
        xs, *st_s = _layer(xs, state_lru_conv[l], state_lru_h[l], state_s5_re[l], state_s5_im[l],
                           state_ffn_conv[l], xs.shape[1], mixer_w, ffn_w)
        outs_s.append(st_s)
    stack = lambda outs, j: jnp.stack([o[j] for o in outs])
    return (xp, xs,
            *(stack(outs_p, j) for j in range(5)),
            *(stack(outs_s, j) for j in range(5)))
```

```python
import functools
import math

import jax
import jax.numpy as jnp
from jax.experimental import pallas as pl
from jax.experimental.pallas import tpu as pltpu

D_MODEL = 1024
LRU_HEADS = 16
LRU_HEAD_DIM = D_MODEL // LRU_HEADS
LRU_CONV = 4
LRU_C = 8.0
S5_GROUP = 16
D_S5 = D_MODEL // 2
S5_GROUPS = D_S5 // S5_GROUP
S5_STATE = 64
D_STATE = S5_GROUPS * S5_STATE
D_FF = 3 * D_MODEL
FFN_CONV = 3
DEPTH = 1
ALPHA = (2.0 * DEPTH) ** 0.25
LN_EPS = 1e-5
LOG2_E = math.log2(math.e)
RSQRT_FLOOR = 1e-36

MXU_TILE = 256
SUBLANES = 8
GATE_BLOCKS = D_MODEL // MXU_TILE
S5_IN_BLOCK = 128
S5_BLOCKS = D_S5 // S5_IN_BLOCK
S5_STATE_BLOCK = D_STATE // S5_BLOCKS
PROMPT_TILE_STEPS = 64
HEAD_ROWS = 256
VMEM_LIMIT_BYTES = 56 * 1024 * 1024

_BF16 = jnp.bfloat16
_F32 = jnp.float32


def _dot(a, b):
    return jnp.dot(a, b, preferred_element_type=_F32)


def _sigmoid_of_twice(half_z):
    return 0.5 * jnp.tanh(half_z) + 0.5


def _layer_norm(y, g, b):
    mu = jnp.mean(y, axis=-1, keepdims=True)
    yc = y - mu
    var = jnp.mean(yc * yc, axis=-1, keepdims=True)
    return yc * jax.lax.rsqrt(var + LN_EPS) * g + b


def _mixer_kernel(nb, tt, n_tiles, batch_major, fresh,
                  x_ref, w_in_ref, cw_ref, cb_ref, wg_ref, ba_ref, bx_ref, lam_ref,
                  wb_ref, are_ref, aim_ref, wcr_ref, wci_ref, d_ref, wglu_ref, wout_ref, *rest):
    initial_state, rest = ((), rest) if fresh else (rest[:4], rest[4:])
    (y1_ref, conv_out_ref, h_ref, sre_ref, sim_ref, xbuf, abuf, hbuf, bure, buim, gpre, *maybe_xt) = rest
    m = nb * tt
    hist = (LRU_CONV - 1) * nb

    @pl.when(pl.program_id(0) == 0)
    def _():
        if fresh:
            xbuf[0:hist, :] = jnp.zeros((hist, D_MODEL), _F32)
            for state in (h_ref, sre_ref, sim_ref):
                state[...] = jnp.zeros(state.shape, _F32)
        else:
            conv0_ref, h0_ref, sre0_ref, sim0_ref = initial_state
            xbuf[0:hist, :] = conv0_ref[...]
            h_ref[...] = h0_ref[...]
            sre_ref[...] = sre0_ref[...]
            sim_ref[...] = sim0_ref[...]

    if batch_major:
        xt, x_sems = maybe_xt
        step = pl.program_id(0)
        slot = step % 2

        def tile_copies(tile, into):
            return [pltpu.make_async_copy(x_ref.at[b, pl.ds(tile * tt, tt), :],
                                          xt.at[into, :, b, :],
                                          x_sems.at[into, b]) for b in range(nb)]

        @pl.when(step == 0)
        def _():
            for copy in tile_copies(0, 0):
                copy.start()

        @pl.when(step + 1 < n_tiles)
        def _():
            for copy in tile_copies(step + 1, 1 - slot):
                copy.start()

        for copy in tile_copies(step, slot):
            copy.wait()
        load_x = lambda: xt[slot].reshape(m, D_MODEL)
    else:
        load_x = lambda: jnp.concatenate([x_ref[:, t, :] for t in range(tt)], axis=0)
    xb = load_x().astype(_BF16)

    lam = lam_ref[...]
    c_lam = (0.5 * LRU_C * LOG2_E) * (jnp.minimum(lam, 0.0) - jnp.log1p(jnp.exp(-jnp.abs(lam))))
    gate_cols = D_MODEL + D_S5

    def lru_input_projection(k):
        blk = slice(k * MXU_TILE, (k + 1) * MXU_TILE)
        for r0 in range(0, m, HEAD_ROWS if k == 0 else m):
            rows = slice(r0, r0 + (HEAD_ROWS if k == 0 else m))
            xbuf[hist + rows.start:hist + rows.stop, blk] = _dot(xb[rows, :], w_in_ref[:, blk])

    lru_input_projection(0)
    for k in range(GATE_BLOCKS):
        blk = slice(k * MXU_TILE, (k + 1) * MXU_TILE)
        if k + 1 < GATE_BLOCKS:
            lru_input_projection(k + 1)
        if k == 0:
            u = _dot(xb, w_in_ref[:, D_MODEL:D_MODEL + D_S5])
            ub = u.astype(_BF16)
            for j in range(S5_BLOCKS):
                bu = _dot(ub[:, j * S5_IN_BLOCK:(j + 1) * S5_IN_BLOCK], wb_ref[j])
                cols = slice(j * S5_STATE_BLOCK, (j + 1) * S5_STATE_BLOCK)
                bure[:, cols] = bu[:, 0:S5_STATE_BLOCK]
                buim[:, cols] = bu[:, S5_STATE_BLOCK:]
        xc = cb_ref[:, blk] + xbuf[0:m, blk] * cw_ref[0:1, blk]
        for j in range(1, LRU_CONV):
            xc = xc + xbuf[j * nb:j * nb + m, blk] * cw_ref[j:j + 1, blk]
        new_hist = xbuf[m:m + hist, blk]
        xbuf[0:hist, blk] = new_hist
        conv_out_ref[:, blk] = new_hist
        half_g = _dot(xc.astype(_BF16), wg_ref[k])
        tanh_r = jnp.tanh(half_g[:, 0:MXU_TILE] + ba_ref[:, blk])
        ig = _sigmoid_of_twice(half_g[:, MXU_TILE:] + bx_ref[:, blk])
        a = jnp.exp2(tanh_r * c_lam[:, blk] + c_lam[:, blk])
        abuf[:, blk] = a
        one_minus_a2 = jnp.maximum((1.0 - a) * (1.0 + a), 0.0)
        mult = one_minus_a2 * jax.lax.rsqrt(jnp.maximum(one_minus_a2, RSQRT_FLOOR))
        hbuf[:, blk] = mult * (ig * xc)
        for half in range(2):
            gblk = slice((2 * k + half) * MXU_TILE, (2 * k + half + 1) * MXU_TILE)
            gpre[:, gblk] = _dot(xb, w_in_ref[:, gate_cols + gblk.start:gate_cols + gblk.stop])

    h = h_ref[...]
    for t in range(tt):
        rows = slice(t * nb, (t + 1) * nb)
        h = abuf[rows, :] * h + hbuf[rows, :]
        hbuf[rows, :] = h
    h_ref[...] = h

    ys = []
    for k in range(S5_BLOCKS):
        cols = slice(k * S5_STATE_BLOCK, (k + 1) * S5_STATE_BLOCK)
        ar = jnp.broadcast_to(are_ref[:, cols], (nb, S5_STATE_BLOCK))
        ai = jnp.broadcast_to(aim_ref[:, cols], (nb, S5_STATE_BLOCK))
        hr = sre_ref[:, cols]
        hi = sim_ref[:, cols]
        for t in range(tt):
            rows = slice(t * nb, (t + 1) * nb)
            hr_new = ar * hr - ai * hi + bure[rows, cols]
            hi_new = ar * hi + ai * hr + buim[rows, cols]
            hr, hi = hr_new, hi_new
            bure[rows, cols] = hr
            buim[rows, cols] = hi
        sre_ref[:, cols] = hr
        sim_ref[:, cols] = hi
        ys.append(_dot(bure[:, cols].astype(_BF16), wcr_ref[k])
                  - _dot(buim[:, cols].astype(_BF16), wci_ref[k]))
    y = jnp.concatenate(ys, axis=1) + d_ref[...] * u
    z = jax.nn.gelu(y).astype(_BF16)
    glu = _dot(z, wglu_ref[...])
    s5_out = glu[:, 0:D_MODEL] * _sigmoid_of_twice(glu[:, D_MODEL:])

    g_lru = _sigmoid_of_twice(gpre[:, 0:D_MODEL])
    g_s5 = _sigmoid_of_twice(gpre[:, D_MODEL:])
    merged = (g_lru * hbuf[...] + g_s5 * s5_out).astype(_BF16)
    mix = _dot(merged, wout_ref[...])
    y1_ref[...] = ALPHA * load_x() + mix


def _ffn_kernel(nb, tt, n_tiles, batch_major, fresh,
                y1_ref, g1_ref, b1_ref, wup_ref, cw_ref, cb_ref, wdown_ref, g2_ref, b2_ref, *rest):
    initial_state, rest = ((), rest) if fresh else (rest[:1], rest[1:])
    (out_ref, conv_out_ref, abuf, *maybe_yt) = rest
    m = nb * tt
    hist = (FFN_CONV - 1) * nb
    step = pl.program_id(0)

    @pl.when(step == 0)
    def _():
        abuf[0:hist, :] = jnp.zeros((hist, D_FF), _F32) if fresh else initial_state[0][...]

    if batch_major:
        yt, y_sems = maybe_yt
        slot = step % 2

        def tile_copies(tile, staged):
            return [pltpu.make_async_copy(yt.at[staged, :, b, :],
                                          out_ref.at[b, pl.ds(tile * tt, tt), :],
                                          y_sems.at[staged, b]) for b in range(nb)]

        @pl.when(step >= 2)
        def _():
            for copy in tile_copies(step - 2, slot):
                copy.wait()

    x = _layer_norm(y1_ref[...], g1_ref[...], b1_ref[...])
    xb = x.astype(_BF16)
    for r0 in range(0, m, HEAD_ROWS):
        abuf[hist + r0:hist + r0 + HEAD_ROWS, 0:D_MODEL] = _dot(xb[r0:r0 + HEAD_ROWS, :], wup_ref[:, 0:D_MODEL])
    abuf[hist:hist + m, D_MODEL:] = _dot(xb, wup_ref[:, D_MODEL:D_FF])
    f = None
    for c in range(D_FF // D_MODEL):
        cols = slice(c * D_MODEL, (c + 1) * D_MODEL)
        ac = cb_ref[:, cols] + abuf[0:m, cols] * cw_ref[0:1, cols]
        for k in range(1, FFN_CONV):
            ac = ac + abuf[k * nb:k * nb + m, cols] * cw_ref[k:k + 1, cols]
        gate = _dot(xb, wup_ref[:, D_FF + c * D_MODEL:D_FF + (c + 1) * D_MODEL])
        hmid = (jax.nn.gelu(ac) * gate).astype(_BF16)
        part = _dot(hmid, wdown_ref[cols, :])
        f = part if f is None else f + part
    new_hist = abuf[m:m + hist, :]
    abuf[0:hist, :] = new_hist
    conv_out_ref[...] = new_hist
    y = _layer_norm(ALPHA * x + f, g2_ref[...], b2_ref[...])
    if batch_major:
        yt[slot] = y.reshape(tt, nb, D_MODEL)
        for copy in tile_copies(step, slot):
            copy.start()

        @pl.when(step == n_tiles - 1)
        def _():
            if n_tiles > 1:
                for copy in tile_copies(step - 1, 1 - slot):
                    copy.wait()
            for copy in tile_copies(step, slot):
                copy.wait()
    else:
        for t in range(tt):
            out_ref[:, t, :] = y[t * nb:(t + 1) * nb, :]


def _resident(shape):
    zeros = (0,) * len(shape)
    return pl.BlockSpec(shape, lambda i: zeros, pipeline_mode=pl.Buffered(1))


def _run_mixer(x, nb, tt, weights, states, batch_major):
    m = nb * tt
    hist = (LRU_CONV - 1) * nb
    tile = pl.BlockSpec((m, D_MODEL), lambda i: (i, 0))
    if batch_major:
        n_rows = nb * x.shape[1]
        x_spec = pl.BlockSpec(memory_space=pl.ANY)
    else:
        n_rows = nb * x.shape[1]
        x_spec = pl.BlockSpec((nb, tt, D_MODEL), lambda i: (0, i, 0))
    operands = (x,) + tuple(weights) + tuple(states or ())
    in_specs = [x_spec] + [_resident(a.shape) for a in operands[1:]]
    out_shape = (
        jax.ShapeDtypeStruct((n_rows, D_MODEL), _F32),
        jax.ShapeDtypeStruct((hist, D_MODEL), _F32),
        jax.ShapeDtypeStruct((nb, D_MODEL), _F32),
        jax.ShapeDtypeStruct((nb, D_STATE), _F32),
        jax.ShapeDtypeStruct((nb, D_STATE), _F32),
    )
    out_specs = [tile] + [pl.BlockSpec(s.shape, lambda i: (0, 0)) for s in out_shape[1:]]
    scratch_shapes = [
        pltpu.VMEM((hist + m, D_MODEL), _F32),
        pltpu.VMEM((m, D_MODEL), _F32),
        pltpu.VMEM((m, D_MODEL), _F32),
        pltpu.VMEM((m, D_STATE), _F32),
        pltpu.VMEM((m, D_STATE), _F32),
        pltpu.VMEM((m, 2 * D_MODEL), _F32),
    ]
    if batch_major:
        scratch_shapes += [pltpu.VMEM((2, tt, nb, D_MODEL), _F32), pltpu.SemaphoreType.DMA((2, nb))]
    return pl.pallas_call(
        functools.partial(_mixer_kernel, nb, tt, n_rows // m, batch_major, states is None),
        out_shape=out_shape,
        grid=(n_rows // m,),
        in_specs=in_specs,
        out_specs=out_specs,
        scratch_shapes=scratch_shapes,
        compiler_params=pltpu.CompilerParams(
            dimension_semantics=("arbitrary",), vmem_limit_bytes=VMEM_LIMIT_BYTES),
        name=f"mixer_nb{nb}",
    )(*operands)


def _run_ffn(x_rows, nb, tt, weights, conv0, batch_major):
    n_rows = x_rows.shape[0]
    m = nb * tt
    hist = (FFN_CONV - 1) * nb
    tile = pl.BlockSpec((m, D_MODEL), lambda i: (i, 0))
    operands = (x_rows,) + tuple(weights) + (() if conv0 is None else (conv0,))
    in_specs = [tile] + [_resident(a.shape) for a in operands[1:]]
    y_shape = jax.ShapeDtypeStruct((nb, n_rows // nb, D_MODEL), _F32)
    if batch_major:
        y_spec = pl.BlockSpec(memory_space=pl.ANY)
    else:
        y_spec = pl.BlockSpec((nb, tt, D_MODEL), lambda i: (0, i, 0))
    out_shape = (y_shape, jax.ShapeDtypeStruct((hist, D_FF), _F32))
    out_specs = [y_spec, pl.BlockSpec((hist, D_FF), lambda i: (0, 0))]
    scratch_shapes = [pltpu.VMEM((hist + m, D_FF), _F32)]
    if batch_major:
        scratch_shapes += [pltpu.VMEM((2, tt, nb, D_MODEL), _F32), pltpu.SemaphoreType.DMA((2, nb))]
    return pl.pallas_call(
        functools.partial(_ffn_kernel, nb, tt, n_rows // m, batch_major, conv0 is None),
        out_shape=out_shape,
        grid=(n_rows // m,),
        in_specs=in_specs,
        out_specs=out_specs,
        scratch_shapes=scratch_shapes,
        compiler_params=pltpu.CompilerParams(
            dimension_semantics=("arbitrary",), vmem_limit_bytes=VMEM_LIMIT_BYTES),
        name=f"ffn_nb{nb}",
    )(*operands)


def _block_diag(blocks, n_per_tile):
    n, r, c = blocks.shape
    tiles = n // n_per_tile
    eye = jnp.eye(n_per_tile, dtype=blocks.dtype)
    b = blocks.reshape(tiles, n_per_tile, r, c)
    return jnp.einsum("tgrc,gh->tgrhc", b, eye).reshape(tiles, n_per_tile * r, n_per_tile * c)


def _s5_discretise(a_re, a_im, log_dt, b_re, b_im):
    dt = jnp.exp(log_dt)[:, None]
    mag = jnp.exp(a_re * dt)
    ab_re = mag * jnp.cos(a_im * dt)
    ab_im = mag * jnp.sin(a_im * dt)
    nr = ab_re - 1.0
    ni = ab_im
    den = a_re * a_re + a_im * a_im
    coef_re = (nr * a_re + ni * a_im) / den
    coef_im = (ni * a_re - nr * a_im) / den
    bb_re = coef_re[..., None] * b_re - coef_im[..., None] * b_im
    bb_im = coef_re[..., None] * b_im + coef_im[..., None] * b_re
    return ab_re, ab_im, bb_re, bb_im


def _time_major(x):
    b, l, c = x.shape
    return jnp.transpose(x, (1, 0, 2)).reshape(l * b, c)


def _batch_major(rows, b):
    n, c = rows.shape
    return jnp.transpose(rows.reshape(n // b, b, c), (1, 0, 2))


def _layer(x, state, tt, mixer_w, ffn_w):
    nb = x.shape[0]
    batch_major = tt % SUBLANES == 0
    mixer_state = ffn_conv0 = None
    if state is not None:
        lru_conv0, lru_h0, s5_re0, s5_im0, ffn_conv = state
        mixer_state = (_time_major(lru_conv0), lru_h0, s5_re0.reshape(nb, D_STATE), s5_im0.reshape(nb, D_STATE))
        ffn_conv0 = _time_major(ffn_conv)
    x1, conv_out, h_out, sre, sim = _run_mixer(x, nb, tt, mixer_w, mixer_state, batch_major)
    y, fconv_out = _run_ffn(x1, nb, tt, ffn_w, ffn_conv0, batch_major)
    return (y, _batch_major(conv_out, nb), h_out,
            sre.reshape(nb, S5_GROUPS, S5_STATE), sim.reshape(nb, S5_GROUPS, S5_STATE),
            _batch_major(fconv_out, nb))


def kernel(x_prompt, x_sample, state_lru_conv, state_lru_h, state_s5_re, state_s5_im, state_ffn_conv, w_in, lru_conv_w, lru_conv_b, lru_wa, lru_ba, lru_wx, lru_bx, lru_lambda, s5_a_re, s5_a_im, s5_log_dt, s5_b_re, s5_b_im, s5_c_re, s5_c_im, s5_d, w_glu, w_out, ln1_g, ln1_b, w_up, ffn_conv_w, ffn_conv_b, w_down, ln2_g, ln2_b):
    xp, xs = x_prompt, x_sample
    outs_p, outs_s = [], []
    for l in range(DEPTH):
        row = lambda v: v[l].reshape(1, -1)
        heads_per_tile = MXU_TILE // LRU_HEAD_DIM
        wg = (0.5 * jnp.concatenate([_block_diag(lru_wa[l], heads_per_tile),
                                     _block_diag(lru_wx[l], heads_per_tile)], axis=2)).astype(_BF16)
        w_in_scale = jnp.concatenate([jnp.ones((D_MODEL + D_S5,), _F32), jnp.full((2 * D_MODEL,), 0.5, _F32)])
        w_glu_scale = jnp.concatenate([jnp.ones((D_MODEL,), _F32), jnp.full((D_MODEL,), 0.5, _F32)])
        ab_re, ab_im, bb_re, bb_im = _s5_discretise(s5_a_re[l], s5_a_im[l], s5_log_dt[l], s5_b_re[l], s5_b_im[l])
        groups_per_block = S5_IN_BLOCK // S5_GROUP
        to_in = lambda bb: _block_diag(jnp.swapaxes(bb, 1, 2), groups_per_block)
        to_out = lambda cc: _block_diag(jnp.swapaxes(cc, 1, 2), groups_per_block)
        wb = jnp.concatenate([to_in(bb_re), to_in(bb_im)], axis=2).astype(_BF16)
        mixer_w = (
            (w_in[l] * w_in_scale).astype(_BF16), lru_conv_w[l], row(lru_conv_b), wg,
            0.5 * row(lru_ba), 0.5 * row(lru_bx), row(lru_lambda),
            wb, ab_re.reshape(1, D_STATE), ab_im.reshape(1, D_STATE),
            to_out(s5_c_re[l]).astype(_BF16), to_out(s5_c_im[l]).astype(_BF16), row(s5_d),
            (w_glu[l] * w_glu_scale).astype(_BF16), w_out[l].astype(_BF16),
        )
        ffn_w = (row(ln1_g), row(ln1_b), w_up[l].astype(_BF16), ffn_conv_w[l], row(ffn_conv_b), w_down[l].astype(_BF16),
                 row(ln2_g), row(ln2_b))

        xp, *st_p = _layer(xp, None, PROMPT_TILE_STEPS, mixer_w, ffn_w)
        outs_p.append(st_p)
        carried = (state_lru_conv[l], state_lru_h[l], state_s5_re[l], state_s5_im[l], state_ffn_conv[l])
        xs, *st_s = _layer(xs, carried, xs.shape[1], mixer_w, ffn_w)
        outs_s.append(st_s)
    stack = lambda outs, j: jnp.stack([o[j] for o in outs])
    return (xp, xs,
            *(stack(outs_p, j) for j in range(5)),
            *(stack(outs_s, j) for j in range(5)))
```

```python
import functools
import math

import jax
import jax.numpy as jnp
from jax.experimental import pallas as pl
from jax.experimental.pallas import tpu as pltpu

D_MODEL = 1024
LRU_HEADS = 16
LRU_HEAD_DIM = D_MODEL // LRU_HEADS
LRU_CONV = 4
LRU_C = 8.0
S5_GROUP = 16
D_S5 = D_MODEL // 2
S5_GROUPS = D_S5 // S5_GROUP
S5_STATE = 64
D_STATE = S5_GROUPS * S5_STATE
D_FF = 3 * D_MODEL
FFN_CONV = 3
DEPTH = 1
ALPHA = (2.0 * DEPTH) ** 0.25
LN_EPS = 1e-5
LOG2_E = math.log2(math.e)
RSQRT_FLOOR = 1e-36

MXU_TILE = 256
SUBLANES = 8
GATE_BLOCKS = D_MODEL // MXU_TILE
S5_IN_BLOCK = 128
S5_BLOCKS = D_S5 // S5_IN_BLOCK
S5_STATE_BLOCK = D_STATE // S5_BLOCKS
PROMPT_TILE_STEPS = 64
HEAD_ROWS = 256
VMEM_LIMIT_BYTES = 56 * 1024 * 1024

_BF16 = jnp.bfloat16
_F32 = jnp.float32


def _dot(a, b):
    return jnp.dot(a, b, preferred_element_type=_F32)


def _sigmoid_of_twice(half_z):
    return 0.5 * jnp.tanh(half_z) + 0.5


def _layer_norm(y, g, b):
    mu = jnp.mean(y, axis=-1, keepdims=True)
    yc = y - mu
    var = jnp.mean(yc * yc, axis=-1, keepdims=True)
    return yc * jax.lax.rsqrt(var + LN_EPS) * g + b


def _mixer_kernel(nb, tt, n_tiles, batch_major, fresh,
                  x_ref, w_in_ref, cw_ref, cb_ref, wg_ref, ba_ref, bx_ref, lam_ref,
                  wb_ref, are_ref, aim_ref, wcr_ref, wci_ref, d_ref, wglu_ref, wout_ref, *rest):
    initial_state, rest = ((), rest) if fresh else (rest[:4], rest[4:])
    (y1_ref, conv_out_ref, h_ref, sre_ref, sim_ref,
     xbuf, abuf, hbuf, bure, buim, sbre, sbim, gpre, *maybe_xt) = rest
    m = nb * tt
    hist = (LRU_CONV - 1) * nb

    @pl.when(pl.program_id(0) == 0)
    def _():
        if fresh:
            xbuf[0:hist, :] = jnp.zeros((hist, D_MODEL), _F32)
            for state in (h_ref, sre_ref, sim_ref):
                state[...] = jnp.zeros(state.shape, _F32)
        else:
            conv0_ref, h0_ref, sre0_ref, sim0_ref = initial_state
            xbuf[0:hist, :] = conv0_ref[...]
            h_ref[...] = h0_ref[...]
            sre_ref[...] = sre0_ref[...]
            sim_ref[...] = sim0_ref[...]

    if batch_major:
        xt, x_sems = maybe_xt
        step = pl.program_id(0)
        slot = step % 2

        def tile_copies(tile, into):
            return [pltpu.make_async_copy(x_ref.at[b, pl.ds(tile * tt, tt), :],
                                          xt.at[into, :, b, :],
                                          x_sems.at[into, b]) for b in range(nb)]

        @pl.when(step == 0)
        def _():
            for copy in tile_copies(0, 0):
                copy.start()

        @pl.when(step + 1 < n_tiles)
        def _():
            for copy in tile_copies(step + 1, 1 - slot):
                copy.start()

        for copy in tile_copies(step, slot):
            copy.wait()
        load_x = lambda: xt[slot].reshape(m, D_MODEL)
    else:
        load_x = lambda: jnp.concatenate([x_ref[:, t, :] for t in range(tt)], axis=0)
    xb = load_x().astype(_BF16)

    lam = lam_ref[...]
    c_lam = (0.5 * LRU_C * LOG2_E) * (jnp.minimum(lam, 0.0) - jnp.log1p(jnp.exp(-jnp.abs(lam))))
    gate_cols = D_MODEL + D_S5

    def lru_input_projection(k):
        blk = slice(k * MXU_TILE, (k + 1) * MXU_TILE)
        for r0 in range(0, m, HEAD_ROWS if k == 0 else m):
            rows = slice(r0, r0 + (HEAD_ROWS if k == 0 else m))
            xbuf[hist + rows.start:hist + rows.stop, blk] = _dot(xb[rows, :], w_in_ref[:, blk])

    lru_input_projection(0)
    for k in range(GATE_BLOCKS):
        blk = slice(k * MXU_TILE, (k + 1) * MXU_TILE)
        if k + 1 < GATE_BLOCKS:
            lru_input_projection(k + 1)
        if k == 0:
            u = _dot(xb, w_in_ref[:, D_MODEL:D_MODEL + D_S5])
            ub = u.astype(_BF16)
            for j in range(S5_BLOCKS):
                bu = _dot(ub[:, j * S5_IN_BLOCK:(j + 1) * S5_IN_BLOCK], wb_ref[j])
                cols = slice(j * S5_STATE_BLOCK, (j + 1) * S5_STATE_BLOCK)
                bure[:, cols] = bu[:, 0:S5_STATE_BLOCK]
                buim[:, cols] = bu[:, S5_STATE_BLOCK:]
        xc = cb_ref[:, blk] + xbuf[0:m, blk] * cw_ref[0:1, blk]
        for j in range(1, LRU_CONV):
            xc = xc + xbuf[j * nb:j * nb + m, blk] * cw_ref[j:j + 1, blk]
        new_hist = xbuf[m:m + hist, blk]
        xbuf[0:hist, blk] = new_hist
        conv_out_ref[:, blk] = new_hist
        half_g = _dot(xc.astype(_BF16), wg_ref[k])
        tanh_r = jnp.tanh(half_g[:, 0:MXU_TILE] + ba_ref[:, blk])
        ig = _sigmoid_of_twice(half_g[:, MXU_TILE:] + bx_ref[:, blk])
        a = jnp.exp2(tanh_r * c_lam[:, blk] + c_lam[:, blk])
        abuf[:, blk] = a
        one_minus_a2 = jnp.maximum((1.0 - a) * (1.0 + a), 0.0)
        mult = one_minus_a2 * jax.lax.rsqrt(jnp.maximum(one_minus_a2, RSQRT_FLOOR))
        hbuf[:, blk] = mult * (ig * xc)
        for half in range(2):
            gblk = slice((2 * k + half) * MXU_TILE, (2 * k + half + 1) * MXU_TILE)
            gpre[:, gblk] = _dot(xb, w_in_ref[:, gate_cols + gblk.start:gate_cols + gblk.stop])

    h = h_ref[...]
    for t in range(tt):
        rows = slice(t * nb, (t + 1) * nb)
        h = abuf[rows, :] * h + hbuf[rows, :]
        hbuf[rows, :] = h
    h_ref[...] = h

    ys = []
    for k in range(S5_BLOCKS):
        cols = slice(k * S5_STATE_BLOCK, (k + 1) * S5_STATE_BLOCK)
        ar = jnp.broadcast_to(are_ref[:, cols], (nb, S5_STATE_BLOCK))
        ai = jnp.broadcast_to(aim_ref[:, cols], (nb, S5_STATE_BLOCK))
        hr = sre_ref[:, cols]
        hi = sim_ref[:, cols]
        for t in range(0, tt, 2):
            pair_re, pair_im = [], []
            for rows in (slice(t * nb, (t + 1) * nb), slice((t + 1) * nb, (t + 2) * nb)):
                hr_new = ar * hr - ai * hi + bure[rows, cols]
                hi_new = ar * hi + ai * hr + buim[rows, cols]
                hr, hi = hr_new, hi_new
                pair_re.append(hr)
                pair_im.append(hi)
            pair_rows = slice(t * nb, (t + 2) * nb)
            sbre[pair_rows, cols] = jnp.concatenate(pair_re, axis=0).astype(_BF16)
            sbim[pair_rows, cols] = jnp.concatenate(pair_im, axis=0).astype(_BF16)
        sre_ref[:, cols] = hr
        sim_ref[:, cols] = hi
        ys.append(_dot(sbre[:, cols], wcr_ref[k]) - _dot(sbim[:, cols], wci_ref[k]))
    y = jnp.concatenate(ys, axis=1) + d_ref[...] * u
    z = jax.nn.gelu(y).astype(_BF16)
    glu = _dot(z, wglu_ref[...])
    s5_out = glu[:, 0:D_MODEL] * _sigmoid_of_twice(glu[:, D_MODEL:])

    g_lru = _sigmoid_of_twice(gpre[:, 0:D_MODEL])
    g_s5 = _sigmoid_of_twice(gpre[:, D_MODEL:])
    merged = (g_lru * hbuf[...] + g_s5 * s5_out).astype(_BF16)
    mix = _dot(merged, wout_ref[...])
    y1_ref[...] = ALPHA * load_x() + mix


def _ffn_kernel(nb, tt, n_tiles, batch_major, fresh,
                y1_ref, g1_ref, b1_ref, wup_ref, cw_ref, cb_ref, wdown_ref, g2_ref, b2_ref, *rest):
    initial_state, rest = ((), rest) if fresh else (rest[:1], rest[1:])
    (out_ref, conv_out_ref, abuf, *maybe_yt) = rest
    m = nb * tt
    hist = (FFN_CONV - 1) * nb
    step = pl.program_id(0)

    @pl.when(step == 0)
    def _():
        abuf[0:hist, :] = jnp.zeros((hist, D_FF), _F32) if fresh else initial_state[0][...]

    if batch_major:
        yt, y_sems = maybe_yt
        slot = step % 2

        def tile_copies(tile, staged):
            return [pltpu.make_async_copy(yt.at[staged, :, b, :],
                                          out_ref.at[b, pl.ds(tile * tt, tt), :],
                                          y_sems.at[staged, b]) for b in range(nb)]

        @pl.when(step >= 2)
        def _():
            for copy in tile_copies(step - 2, slot):
                copy.wait()

    x = _layer_norm(y1_ref[...], g1_ref[...], b1_ref[...])
    xb = x.astype(_BF16)
    for r0 in range(0, m, HEAD_ROWS):
        abuf[hist + r0:hist + r0 + HEAD_ROWS, 0:D_MODEL] = _dot(xb[r0:r0 + HEAD_ROWS, :], wup_ref[:, 0:D_MODEL])
    abuf[hist:hist + m, D_MODEL:] = _dot(xb, wup_ref[:, D_MODEL:D_FF])
    f = None
    for c in range(D_FF // D_MODEL):
        cols = slice(c * D_MODEL, (c + 1) * D_MODEL)
        ac = cb_ref[:, cols] + abuf[0:m, cols] * cw_ref[0:1, cols]
        for k in range(1, FFN_CONV):
            ac = ac + abuf[k * nb:k * nb + m, cols] * cw_ref[k:k + 1, cols]
        gate = _dot(xb, wup_ref[:, D_FF + c * D_MODEL:D_FF + (c + 1) * D_MODEL])
        hmid = (jax.nn.gelu(ac) * gate).astype(_BF16)
        part = _dot(hmid, wdown_ref[cols, :])
        f = part if f is None else f + part
    new_hist = abuf[m:m + hist, :]
    abuf[0:hist, :] = new_hist
    conv_out_ref[...] = new_hist
    y = _layer_norm(ALPHA * x + f, g2_ref[...], b2_ref[...])
    if batch_major:
        yt[slot] = y.reshape(tt, nb, D_MODEL)
        for copy in tile_copies(step, slot):
            copy.start()

        @pl.when(step == n_tiles - 1)
        def _():
            if n_tiles > 1:
                for copy in tile_copies(step - 1, 1 - slot):
                    copy.wait()
            for copy in tile_copies(step, slot):
                copy.wait()
    else:
        for t in range(tt):
            out_ref[:, t, :] = y[t * nb:(t + 1) * nb, :]


def _resident(shape):
    zeros = (0,) * len(shape)
    return pl.BlockSpec(shape, lambda i: zeros, pipeline_mode=pl.Buffered(1))


def _run_mixer(x, nb, tt, weights, states, batch_major):
    m = nb * tt
    hist = (LRU_CONV - 1) * nb
    tile = pl.BlockSpec((m, D_MODEL), lambda i: (i, 0))
    if batch_major:
        n_rows = nb * x.shape[1]
        x_spec = pl.BlockSpec(memory_space=pl.ANY)
    else:
        n_rows = nb * x.shape[1]
        x_spec = pl.BlockSpec((nb, tt, D_MODEL), lambda i: (0, i, 0))
    operands = (x,) + tuple(weights) + tuple(states or ())
    in_specs = [x_spec] + [_resident(a.shape) for a in operands[1:]]
    out_shape = (
        jax.ShapeDtypeStruct((n_rows, D_MODEL), _F32),
        jax.ShapeDtypeStruct((hist, D_MODEL), _F32),
        jax.ShapeDtypeStruct((nb, D_MODEL), _F32),
        jax.ShapeDtypeStruct((nb, D_STATE), _F32),
        jax.ShapeDtypeStruct((nb, D_STATE), _F32),
    )
    out_specs = [tile] + [pl.BlockSpec(s.shape, lambda i: (0, 0)) for s in out_shape[1:]]
    scratch_shapes = [
        pltpu.VMEM((hist + m, D_MODEL), _F32),
        pltpu.VMEM((m, D_MODEL), _F32),
        pltpu.VMEM((m, D_MODEL), _F32),
        pltpu.VMEM((m, D_STATE), _F32),
        pltpu.VMEM((m, D_STATE), _F32),
        pltpu.VMEM((m, D_STATE), _BF16),
        pltpu.VMEM((m, D_STATE), _BF16),
        pltpu.VMEM((m, 2 * D_MODEL), _F32),
    ]
    if batch_major:
        scratch_shapes += [pltpu.VMEM((2, tt, nb, D_MODEL), _F32), pltpu.SemaphoreType.DMA((2, nb))]
    return pl.pallas_call(
        functools.partial(_mixer_kernel, nb, tt, n_rows // m, batch_major, states is None),
        out_shape=out_shape,
        grid=(n_rows // m,),
        in_specs=in_specs,
        out_specs=out_specs,
        scratch_shapes=scratch_shapes,
        compiler_params=pltpu.CompilerParams(
            dimension_semantics=("arbitrary",), vmem_limit_bytes=VMEM_LIMIT_BYTES),
        name=f"mixer_nb{nb}",
    )(*operands)


def _run_ffn(x_rows, nb, tt, weights, conv0, batch_major):
    n_rows = x_rows.shape[0]
    m = nb * tt
    hist = (FFN_CONV - 1) * nb
    tile = pl.BlockSpec((m, D_MODEL), lambda i: (i, 0))
    operands = (x_rows,) + tuple(weights) + (() if conv0 is None else (conv0,))
    in_specs = [tile] + [_resident(a.shape) for a in operands[1:]]
    y_shape = jax.ShapeDtypeStruct((nb, n_rows // nb, D_MODEL), _F32)
    if batch_major:
        y_spec = pl.BlockSpec(memory_space=pl.ANY)
    else:
        y_spec = pl.BlockSpec((nb, tt, D_MODEL), lambda i: (0, i, 0))
    out_shape = (y_shape, jax.ShapeDtypeStruct((hist, D_FF), _F32))
    out_specs = [y_spec, pl.BlockSpec((hist, D_FF), lambda i: (0, 0))]
    scratch_shapes = [pltpu.VMEM((hist + m, D_FF), _F32)]
    if batch_major:
        scratch_shapes += [pltpu.VMEM((2, tt, nb, D_MODEL), _F32), pltpu.SemaphoreType.DMA((2, nb))]
    return pl.pallas_call(
        functools.partial(_ffn_kernel, nb, tt, n_rows // m, batch_major, conv0 is None),
        out_shape=out_shape,
        grid=(n_rows // m,),
        in_specs=in_specs,
        out_specs=out_specs,
        scratch_shapes=scratch_shapes,
        compiler_params=pltpu.CompilerParams(
            dimension_semantics=("arbitrary",), vmem_limit_bytes=VMEM_LIMIT_BYTES),
        name=f"ffn_nb{nb}",
    )(*operands)


def _block_diag(blocks, n_per_tile):
    n, r, c = blocks.shape
    tiles = n // n_per_tile
    eye = jnp.eye(n_per_tile, dtype=blocks.dtype)
    b = blocks.reshape(tiles, n_per_tile, r, c)
    return jnp.einsum("tgrc,gh->tgrhc", b, eye).reshape(tiles, n_per_tile * r, n_per_tile * c)


def _s5_discretise(a_re, a_im, log_dt, b_re, b_im):
    dt = jnp.exp(log_dt)[:, None]
    mag = jnp.exp(a_re * dt)
    ab_re = mag * jnp.cos(a_im * dt)
    ab_im = mag * jnp.sin(a_im * dt)
    nr = ab_re - 1.0
    ni = ab_im
    den = a_re * a_re + a_im * a_im
    coef_re = (nr * a_re + ni * a_im) / den
    coef_im = (ni * a_re - nr * a_im) / den
    bb_re = coef_re[..., None] * b_re - coef_im[..., None] * b_im
    bb_im = coef_re[..., None] * b_im + coef_im[..., None] * b_re
    return ab_re, ab_im, bb_re, bb_im


def _time_major(x):
    b, l, c = x.shape
    return jnp.transpose(x, (1, 0, 2)).reshape(l * b, c)


def _batch_major(rows, b):
    n, c = rows.shape
    return jnp.transpose(rows.reshape(n // b, b, c), (1, 0, 2))


def _layer(x, state, tt, mixer_w, ffn_w):
    nb = x.shape[0]
    batch_major = tt % SUBLANES == 0
    mixer_state = ffn_conv0 = None
    if state is not None:
        lru_conv0, lru_h0, s5_re0, s5_im0, ffn_conv = state
        mixer_state = (_time_major(lru_conv0), lru_h0, s5_re0.reshape(nb, D_STATE), s5_im0.reshape(nb, D_STATE))
        ffn_conv0 = _time_major(ffn_conv)
    x1, conv_out, h_out, sre, sim = _run_mixer(x, nb, tt, mixer_w, mixer_state, batch_major)
    y, fconv_out = _run_ffn(x1, nb, tt, ffn_w, ffn_conv0, batch_major)
    return (y, _batch_major(conv_out, nb), h_out,
            sre.reshape(nb, S5_GROUPS, S5_STATE), sim.reshape(nb, S5_GROUPS, S5_STATE),
            _batch_major(fconv_out, nb))


def kernel(x_prompt, x_sample, state_lru_conv, state_lru_h, state_s5_re, state_s5_im, state_ffn_conv, w_in, lru_conv_w, lru_conv_b, lru_wa, lru_ba, lru_wx, lru_bx, lru_lambda, s5_a_re, s5_a_im, s5_log_dt, s5_b_re, s5_b_im, s5_c_re, s5_c_im, s5_d, w_glu, w_out, ln1_g, ln1_b, w_up, ffn_conv_w, ffn_conv_b, w_down, ln2_g, ln2_b):
    xp, xs = x_prompt, x_sample
    outs_p, outs_s = [], []
    for l in range(DEPTH):
        row = lambda v: v[l].reshape(1, -1)
        heads_per_tile = MXU_TILE // LRU_HEAD_DIM
        wg = (0.5 * jnp.concatenate([_block_diag(lru_wa[l], heads_per_tile),
                                     _block_diag(lru_wx[l], heads_per_tile)], axis=2)).astype(_BF16)
        w_in_scale = jnp.concatenate([jnp.ones((D_MODEL + D_S5,), _F32), jnp.full((2 * D_MODEL,), 0.5, _F32)])
        w_glu_scale = jnp.concatenate([jnp.ones((D_MODEL,), _F32), jnp.full((D_MODEL,), 0.5, _F32)])
        ab_re, ab_im, bb_re, bb_im = _s5_discretise(s5_a_re[l], s5_a_im[l], s5_log_dt[l], s5_b_re[l], s5_b_im[l])
        groups_per_block = S5_IN_BLOCK // S5_GROUP
        to_in = lambda bb: _block_diag(jnp.swapaxes(bb, 1, 2), groups_per_block)
        to_out = lambda cc: _block_diag(jnp.swapaxes(cc, 1, 2), groups_per_block)
        wb = jnp.concatenate([to_in(bb_re), to_in(bb_im)], axis=2).astype(_BF16)
        mixer_w = (
            (w_in[l] * w_in_scale).astype(_BF16), lru_conv_w[l], row(lru_conv_b), wg,
            0.5 * row(lru_ba), 0.5 * row(lru_bx), row(lru_lambda),
            wb, ab_re.reshape(1, D_STATE), ab_im.reshape(1, D_STATE),
            to_out(s5_c_re[l]).astype(_BF16), to_out(s5_c_im[l]).astype(_BF16), row(s5_d),
            (w_glu[l] * w_glu_scale).astype(_BF16), w_out[l].astype(_BF16),
        )
        ffn_w = (row(ln1_g), row(ln1_b), w_up[l].astype(_BF16), ffn_conv_w[l], row(ffn_conv_b), w_down[l].astype(_BF16),
                 row(ln2_g), row(ln2_b))

        xp, *st_p = _layer(xp, None, PROMPT_TILE_STEPS, mixer_w, ffn_w)
        outs_p.append(st_p)
        carried = (state_lru_conv[l], state_lru_h[l], state_s5_re[l], state_s5_im[l], state_ffn_conv[l])
        xs, *st_s = _layer(xs, carried, xs.shape[1], mixer_w, ffn_w)
        outs_s.append(st_s)
    stack = lambda outs, j: jnp.stack([o[j] for o in outs])
    return (xp, xs,
            *(stack(outs_p, j) for j in range(5)),
            *(stack(outs_s, j) for j in range(5)))
```

```python
import functools
import math

import jax
import jax.numpy as jnp
from jax.experimental import pallas as pl
from jax.experimental.pallas import tpu as pltpu

D_MODEL = 1024
LRU_HEADS = 16
LRU_HEAD_DIM = D_MODEL // LRU_HEADS
LRU_CONV = 4
LRU_C = 8.0
S5_GROUP = 16
D_S5 = D_MODEL // 2
S5_GROUPS = D_S5 // S5_GROUP
S5_STATE = 64
D_STATE = S5_GROUPS * S5_STATE
D_FF = 3 * D_MODEL
FFN_CONV = 3
DEPTH = 1
ALPHA = (2.0 * DEPTH) ** 0.25
LN_EPS = 1e-5
LOG2_E = math.log2(math.e)
RSQRT_FLOOR = 1e-36

MXU_TILE = 256
SUBLANES = 8
GATE_BLOCKS = D_MODEL // MXU_TILE
S5_IN_BLOCK = 128
S5_BLOCKS = D_S5 // S5_IN_BLOCK
S5_STATE_BLOCK = D_STATE // S5_BLOCKS
PROMPT_TILE_STEPS = 64
FFN_TILE_FACTOR = 2
HEAD_ROWS = 256
VMEM_LIMIT_BYTES = 60 * 1024 * 1024

_BF16 = jnp.bfloat16
_F32 = jnp.float32


def _dot(a, b):
    return jnp.dot(a, b, preferred_element_type=_F32)


def _sigmoid_of_twice(half_z):
    return 0.5 * jnp.tanh(half_z) + 0.5


def _layer_norm(y, g, b):
    mu = jnp.mean(y, axis=-1, keepdims=True)
    yc = y - mu
    var = jnp.mean(yc * yc, axis=-1, keepdims=True)
    return yc * jax.lax.rsqrt(var + LN_EPS) * g + b


def _mixer_kernel(nb, tt, n_tiles, batch_major, fresh,
                  x_ref, w_in_ref, cw_ref, cb_ref, wg_ref, ba_ref, bx_ref, lam_ref,
                  wb_ref, are_ref, aim_ref, wcr_ref, wci_ref, d_ref, wglu_ref, wout_ref, *rest):
    initial_state, rest = ((), rest) if fresh else (rest[:4], rest[4:])
    (y1_ref, conv_out_ref, h_ref, sre_ref, sim_ref, xbuf, abuf, hbuf, bure, buim, gpre, *maybe_xt) = rest
    m = nb * tt
    hist = (LRU_CONV - 1) * nb

    @pl.when(pl.program_id(0) == 0)
    def _():
        if fresh:
            xbuf[0:hist, :] = jnp.zeros((hist, D_MODEL), _F32)
            for state in (h_ref, sre_ref, sim_ref):
                state[...] = jnp.zeros(state.shape, _F32)
        else:
            conv0_ref, h0_ref, sre0_ref, sim0_ref = initial_state
            xbuf[0:hist, :] = conv0_ref[...]
            h_ref[...] = h0_ref[...]
            sre_ref[...] = sre0_ref[...]
            sim_ref[...] = sim0_ref[...]

    if batch_major:
        xt, x_sems = maybe_xt
        step = pl.program_id(0)
        slot = step % 2

        def tile_copies(tile, into):
            return [pltpu.make_async_copy(x_ref.at[b, pl.ds(tile * tt, tt), :],
                                          xt.at[into, :, b, :],
                                          x_sems.at[into, b]) for b in range(nb)]

        @pl.when(step == 0)
        def _():
            for copy in tile_copies(0, 0):
                copy.start()

        @pl.when(step + 1 < n_tiles)
        def _():
            for copy in tile_copies(step + 1, 1 - slot):
                copy.start()

        for copy in tile_copies(step, slot):
            copy.wait()
        load_x = lambda: xt[slot].reshape(m, D_MODEL)
    else:
        load_x = lambda: jnp.concatenate([x_ref[:, t, :] for t in range(tt)], axis=0)
    xb = load_x().astype(_BF16)

    lam = lam_ref[...]
    c_lam = (0.5 * LRU_C * LOG2_E) * (jnp.minimum(lam, 0.0) - jnp.log1p(jnp.exp(-jnp.abs(lam))))
    gate_cols = D_MODEL + D_S5

    def lru_input_projection(k):
        blk = slice(k * MXU_TILE, (k + 1) * MXU_TILE)
        for r0 in range(0, m, HEAD_ROWS if k == 0 else m):
            rows = slice(r0, r0 + (HEAD_ROWS if k == 0 else m))
            xbuf[hist + rows.start:hist + rows.stop, blk] = _dot(xb[rows, :], w_in_ref[:, blk])

    lru_input_projection(0)
    for k in range(GATE_BLOCKS):
        blk = slice(k * MXU_TILE, (k + 1) * MXU_TILE)
        if k + 1 < GATE_BLOCKS:
            lru_input_projection(k + 1)
        if k == 0:
            u = _dot(xb, w_in_ref[:, D_MODEL:D_MODEL + D_S5])
            ub = u.astype(_BF16)
            for j in range(S5_BLOCKS):
                bu = _dot(ub[:, j * S5_IN_BLOCK:(j + 1) * S5_IN_BLOCK], wb_ref[j])
                cols = slice(j * S5_STATE_BLOCK, (j + 1) * S5_STATE_BLOCK)
                bure[:, cols] = bu[:, 0:S5_STATE_BLOCK]
                buim[:, cols] = bu[:, S5_STATE_BLOCK:]
        xc = cb_ref[:, blk] + xbuf[0:m, blk] * cw_ref[0:1, blk]
        for j in range(1, LRU_CONV):
            xc = xc + xbuf[j * nb:j * nb + m, blk] * cw_ref[j:j + 1, blk]
        new_hist = xbuf[m:m + hist, blk]
        xbuf[0:hist, blk] = new_hist
        conv_out_ref[:, blk] = new_hist
        half_g = _dot(xc.astype(_BF16), wg_ref[k])
        tanh_r = jnp.tanh(half_g[:, 0:MXU_TILE] + ba_ref[:, blk])
        ig = _sigmoid_of_twice(half_g[:, MXU_TILE:] + bx_ref[:, blk])
        a = jnp.exp2(tanh_r * c_lam[:, blk] + c_lam[:, blk])
        abuf[:, blk] = a
        one_minus_a2 = jnp.maximum((1.0 - a) * (1.0 + a), 0.0)
        mult = one_minus_a2 * jax.lax.rsqrt(jnp.maximum(one_minus_a2, RSQRT_FLOOR))
        hbuf[:, blk] = mult * (ig * xc)
        for half in range(2):
            gblk = slice((2 * k + half) * MXU_TILE, (2 * k + half + 1) * MXU_TILE)
            gpre[:, gblk] = _dot(xb, w_in_ref[:, gate_cols + gblk.start:gate_cols + gblk.stop])

    h = h_ref[...]
    for t in range(tt):
        rows = slice(t * nb, (t + 1) * nb)
        h = abuf[rows, :] * h + hbuf[rows, :]
        hbuf[rows, :] = h
    h_ref[...] = h

    ys = []
    for k in range(S5_BLOCKS):
        cols = slice(k * S5_STATE_BLOCK, (k + 1) * S5_STATE_BLOCK)
        ar = jnp.broadcast_to(are_ref[:, cols], (nb, S5_STATE_BLOCK))
        ai = jnp.broadcast_to(aim_ref[:, cols], (nb, S5_STATE_BLOCK))
        hr = sre_ref[:, cols]
        hi = sim_ref[:, cols]
        for t in range(tt):
            rows = slice(t * nb, (t + 1) * nb)
            hr_new = ar * hr - ai * hi + bure[rows, cols]
            hi_new = ar * hi + ai * hr + buim[rows, cols]
            hr, hi = hr_new, hi_new
            bure[rows, cols] = hr
            buim[rows, cols] = hi
        sre_ref[:, cols] = hr
        sim_ref[:, cols] = hi
        ys.append(_dot(bure[:, cols].astype(_BF16), wcr_ref[k])
                  - _dot(buim[:, cols].astype(_BF16), wci_ref[k]))
    y = jnp.concatenate(ys, axis=1) + d_ref[...] * u
    z = jax.nn.gelu(y).astype(_BF16)
    glu = _dot(z, wglu_ref[...])
    s5_out = glu[:, 0:D_MODEL] * _sigmoid_of_twice(glu[:, D_MODEL:])

    g_lru = _sigmoid_of_twice(gpre[:, 0:D_MODEL])
    g_s5 = _sigmoid_of_twice(gpre[:, D_MODEL:])
    merged = (g_lru * hbuf[...] + g_s5 * s5_out).astype(_BF16)
    mix = _dot(merged, wout_ref[...])
    y1_ref[...] = ALPHA * load_x() + mix


def _ffn_kernel(nb, tt, n_tiles, batch_major, fresh,
                y1_ref, g1_ref, b1_ref, wup_ref, cw_ref, cb_ref, wdown_ref, g2_ref, b2_ref, *rest):
    initial_state, rest = ((), rest) if fresh else (rest[:1], rest[1:])
    (out_ref, conv_out_ref, abuf, *maybe_yt) = rest
    m = nb * tt
    hist = (FFN_CONV - 1) * nb
    step = pl.program_id(0)

    @pl.when(step == 0)
    def _():
        abuf[0:hist, :] = jnp.zeros((hist, D_FF), _F32) if fresh else initial_state[0][...]

    if batch_major:
        yt, y_sems = maybe_yt
        slot = step % 2

        def tile_copies(tile, staged):
            return [pltpu.make_async_copy(yt.at[staged, :, b, :],
                                          out_ref.at[b, pl.ds(tile * tt, tt), :],
                                          y_sems.at[staged, b]) for b in range(nb)]

        @pl.when(step >= 2)
        def _():
            for copy in tile_copies(step - 2, slot):
                copy.wait()

    x = _layer_norm(y1_ref[...], g1_ref[...], b1_ref[...])
    xb = x.astype(_BF16)
    for r0 in range(0, m, HEAD_ROWS):
        abuf[hist + r0:hist + r0 + HEAD_ROWS, 0:D_MODEL] = _dot(xb[r0:r0 + HEAD_ROWS, :], wup_ref[:, 0:D_MODEL])
    abuf[hist:hist + m, D_MODEL:] = _dot(xb, wup_ref[:, D_MODEL:D_FF])
    f = None
    for c in range(D_FF // D_MODEL):
        cols = slice(c * D_MODEL, (c + 1) * D_MODEL)
        ac = cb_ref[:, cols] + abuf[0:m, cols] * cw_ref[0:1, cols]
        for k in range(1, FFN_CONV):
            ac = ac + abuf[k * nb:k * nb + m, cols] * cw_ref[k:k + 1, cols]
        gate = _dot(xb, wup_ref[:, D_FF + c * D_MODEL:D_FF + (c + 1) * D_MODEL])
        hmid = (jax.nn.gelu(ac) * gate).astype(_BF16)
        part = _dot(hmid, wdown_ref[cols, :])
        f = part if f is None else f + part
    new_hist = abuf[m:m + hist, :]
    abuf[0:hist, :] = new_hist
    conv_out_ref[...] = new_hist
    y = _layer_norm(ALPHA * x + f, g2_ref[...], b2_ref[...])
    if batch_major:
        yt[slot] = y.reshape(tt, nb, D_MODEL)
        for copy in tile_copies(step, slot):
            copy.start()

        @pl.when(step == n_tiles - 1)
        def _():
            if n_tiles > 1:
                for copy in tile_copies(step - 1, 1 - slot):
                    copy.wait()
            for copy in tile_copies(step, slot):
                copy.wait()
    else:
        for t in range(tt):
            out_ref[:, t, :] = y[t * nb:(t + 1) * nb, :]


def _resident(shape):
    zeros = (0,) * len(shape)
    return pl.BlockSpec(shape, lambda i: zeros, pipeline_mode=pl.Buffered(1))


def _run_mixer(x, nb, tt, weights, states, batch_major):
    m = nb * tt
    hist = (LRU_CONV - 1) * nb
    tile = pl.BlockSpec((m, D_MODEL), lambda i: (i, 0))
    if batch_major:
        n_rows = nb * x.shape[1]
        x_spec = pl.BlockSpec(memory_space=pl.ANY)
    else:
        n_rows = nb * x.shape[1]
        x_spec = pl.BlockSpec((nb, tt, D_MODEL), lambda i: (0, i, 0))
    operands = (x,) + tuple(weights) + tuple(states or ())
    in_specs = [x_spec] + [_resident(a.shape) for a in operands[1:]]
    out_shape = (
        jax.ShapeDtypeStruct((n_rows, D_MODEL), _F32),
        jax.ShapeDtypeStruct((hist, D_MODEL), _F32),
        jax.ShapeDtypeStruct((nb, D_MODEL), _F32),
        jax.ShapeDtypeStruct((nb, D_STATE), _F32),
        jax.ShapeDtypeStruct((nb, D_STATE), _F32),
    )
    out_specs = [tile] + [pl.BlockSpec(s.shape, lambda i: (0, 0)) for s in out_shape[1:]]
    scratch_shapes = [
        pltpu.VMEM((hist + m, D_MODEL), _F32),
        pltpu.VMEM((m, D_MODEL), _F32),
        pltpu.VMEM((m, D_MODEL), _F32),
        pltpu.VMEM((m, D_STATE), _F32),
        pltpu.VMEM((m, D_STATE), _F32),
        pltpu.VMEM((m, 2 * D_MODEL), _F32),
    ]
    if batch_major:
        scratch_shapes += [pltpu.VMEM((2, tt, nb, D_MODEL), _F32), pltpu.SemaphoreType.DMA((2, nb))]
    return pl.pallas_call(
        functools.partial(_mixer_kernel, nb, tt, n_rows // m, batch_major, states is None),
        out_shape=out_shape,
        grid=(n_rows // m,),
        in_specs=in_specs,
        out_specs=out_specs,
        scratch_shapes=scratch_shapes,
        compiler_params=pltpu.CompilerParams(
            dimension_semantics=("arbitrary",), vmem_limit_bytes=VMEM_LIMIT_BYTES),
        name=f"mixer_nb{nb}",
    )(*operands)


def _run_ffn(x_rows, nb, tt, weights, conv0, batch_major):
    n_rows = x_rows.shape[0]
    m = nb * tt
    hist = (FFN_CONV - 1) * nb
    tile = pl.BlockSpec((m, D_MODEL), lambda i: (i, 0))
    operands = (x_rows,) + tuple(weights) + (() if conv0 is None else (conv0,))
    in_specs = [tile] + [_resident(a.shape) for a in operands[1:]]
    y_shape = jax.ShapeDtypeStruct((nb, n_rows // nb, D_MODEL), _F32)
    if batch_major:
        y_spec = pl.BlockSpec(memory_space=pl.ANY)
    else:
        y_spec = pl.BlockSpec((nb, tt, D_MODEL), lambda i: (0, i, 0))
    out_shape = (y_shape, jax.ShapeDtypeStruct((hist, D_FF), _F32))
    out_specs = [y_spec, pl.BlockSpec((hist, D_FF), lambda i: (0, 0))]
    scratch_shapes = [pltpu.VMEM((hist + m, D_FF), _F32)]
    if batch_major:
        scratch_shapes += [pltpu.VMEM((2, tt, nb, D_MODEL), _F32), pltpu.SemaphoreType.DMA((2, nb))]
    return pl.pallas_call(
        functools.partial(_ffn_kernel, nb, tt, n_rows // m, batch_major, conv0 is None),
        out_shape=out_shape,
        grid=(n_rows // m,),
        in_specs=in_specs,
        out_specs=out_specs,
        scratch_shapes=scratch_shapes,
        compiler_params=pltpu.CompilerParams(
            dimension_semantics=("arbitrary",), vmem_limit_bytes=VMEM_LIMIT_BYTES),
        name=f"ffn_nb{nb}",
    )(*operands)


def _block_diag(blocks, n_per_tile):
    n, r, c = blocks.shape
    tiles = n // n_per_tile
    eye = jnp.eye(n_per_tile, dtype=blocks.dtype)
    b = blocks.reshape(tiles, n_per_tile, r, c)
    return jnp.einsum("tgrc,gh->tgrhc", b, eye).reshape(tiles, n_per_tile * r, n_per_tile * c)


def _s5_discretise(a_re, a_im, log_dt, b_re, b_im):
    dt = jnp.exp(log_dt)[:, None]
    mag = jnp.exp(a_re * dt)
    ab_re = mag * jnp.cos(a_im * dt)
    ab_im = mag * jnp.sin(a_im * dt)
    nr = ab_re - 1.0
    ni = ab_im
    den = a_re * a_re + a_im * a_im
    coef_re = (nr * a_re + ni * a_im) / den
    coef_im = (ni * a_re - nr * a_im) / den
    bb_re = coef_re[..., None] * b_re - coef_im[..., None] * b_im
    bb_im = coef_re[..., None] * b_im + coef_im[..., None] * b_re
    return ab_re, ab_im, bb_re, bb_im


def _time_major(x):
    b, l, c = x.shape
    return jnp.transpose(x, (1, 0, 2)).reshape(l * b, c)


def _batch_major(rows, b):
    n, c = rows.shape
    return jnp.transpose(rows.reshape(n // b, b, c), (1, 0, 2))


def _layer(x, state, tt, mixer_w, ffn_w):
    nb = x.shape[0]
    batch_major = tt % SUBLANES == 0
    mixer_state = ffn_conv0 = None
    if state is not None:
        lru_conv0, lru_h0, s5_re0, s5_im0, ffn_conv = state
        mixer_state = (_time_major(lru_conv0), lru_h0, s5_re0.reshape(nb, D_STATE), s5_im0.reshape(nb, D_STATE))
        ffn_conv0 = _time_major(ffn_conv)
    x1, conv_out, h_out, sre, sim = _run_mixer(x, nb, tt, mixer_w, mixer_state, batch_major)
    y, fconv_out = _run_ffn(x1, nb, tt * FFN_TILE_FACTOR if batch_major else tt, ffn_w, ffn_conv0, batch_major)
    return (y, _batch_major(conv_out, nb), h_out,
            sre.reshape(nb, S5_GROUPS, S5_STATE), sim.reshape(nb, S5_GROUPS, S5_STATE),
            _batch_major(fconv_out, nb))


def kernel(x_prompt, x_sample, state_lru_conv, state_lru_h, state_s5_re, state_s5_im, state_ffn_conv, w_in, lru_conv_w, lru_conv_b, lru_wa, lru_ba, lru_wx, lru_bx, lru_lambda, s5_a_re, s5_a_im, s5_log_dt, s5_b_re, s5_b_im, s5_c_re, s5_c_im, s5_d, w_glu, w_out, ln1_g, ln1_b, w_up, ffn_conv_w, ffn_conv_b, w_down, ln2_g, ln2_b):
    xp, xs = x_prompt, x_sample
    outs_p, outs_s = [], []
    for l in range(DEPTH):
        row = lambda v: v[l].reshape(1, -1)
        heads_per_tile = MXU_TILE // LRU_HEAD_DIM
        wg = (0.5 * jnp.concatenate([_block_diag(lru_wa[l], heads_per_tile),
                                     _block_diag(lru_wx[l], heads_per_tile)], axis=2)).astype(_BF16)
        w_in_scale = jnp.concatenate([jnp.ones((D_MODEL + D_S5,), _F32), jnp.full((2 * D_MODEL,), 0.5, _F32)])
        w_glu_scale = jnp.concatenate([jnp.ones((D_MODEL,), _F32), jnp.full((D_MODEL,), 0.5, _F32)])
        ab_re, ab_im, bb_re, bb_im = _s5_discretise(s5_a_re[l], s5_a_im[l], s5_log_dt[l], s5_b_re[l], s5_b_im[l])
        groups_per_block = S5_IN_BLOCK // S5_GROUP
        to_in = lambda bb: _block_diag(jnp.swapaxes(bb, 1, 2), groups_per_block)
        to_out = lambda cc: _block_diag(jnp.swapaxes(cc, 1, 2), groups_per_block)
        wb = jnp.concatenate([to_in(bb_re), to_in(bb_im)], axis=2).astype(_BF16)
        mixer_w = (
            (w_in[l] * w_in_scale).astype(_BF16), lru_conv_w[l], row(lru_conv_b), wg,
            0.5 * row(lru_ba), 0.5 * row(lru_bx), row(lru_lambda),
            wb, ab_re.reshape(1, D_STATE), ab_im.reshape(1, D_STATE),
            to_out(s5_c_re[l]).astype(_BF16), to_out(s5_c_im[l]).astype(_BF16), row(s5_d),
            (w_glu[l] * w_glu_scale).astype(_BF16), w_out[l].astype(_BF16),
        )
        ffn_w = (row(ln1_g), row(ln1_b), w_up[l].astype(_BF16), ffn_conv_w[l], row(ffn_conv_b), w_down[l].astype(_BF16),
                 row(ln2_g), row(ln2_b))

        xp, *st_p = _layer(xp, None, PROMPT_TILE_STEPS, mixer_w, ffn_w)
        outs_p.append(st_p)
        carried = (state_lru_conv[l], state_lru_h[l], state_s5_re[l], state_s5_im[l], state_ffn_conv[l])
        xs, *st_s = _layer(xs, carried, xs.shape[1], mixer_w, ffn_w)
        outs_s.append(st_s)
    stack = lambda outs, j: jnp.stack([o[j] for o in outs])
    return (xp, xs,
            *(stack(outs_p, j) for j in range(5)),
            *(stack(outs_s, j) for j in range(5)))
```

```python
import functools
import math

import jax
import jax.numpy as jnp
from jax.experimental import pallas as pl
from jax.experimental.pallas import tpu as pltpu

D_MODEL = 1024
LRU_HEADS = 16
LRU_HEAD_DIM = D_MODEL // LRU_HEADS
LRU_CONV = 4
LRU_C = 8.0
S5_GROUP = 16
D_S5 = D_MODEL // 2
S5_GROUPS = D_S5 // S5_GROUP
S5_STATE = 64
D_STATE = S5_GROUPS * S5_STATE
D_FF = 3 * D_MODEL
FFN_CONV = 3
DEPTH = 1
ALPHA = (2.0 * DEPTH) ** 0.25
LN_EPS = 1e-5
LOG2_E = math.log2(math.e)
RSQRT_FLOOR = 1e-36

MXU_TILE = 256
SUBLANES = 8
BF16_SUBLANES = 16
GATE_BLOCKS = D_MODEL // MXU_TILE
S5_IN_BLOCK = 128
S5_BLOCKS = D_S5 // S5_IN_BLOCK
S5_STATE_BLOCK = D_STATE // S5_BLOCKS
PROMPT_TILE_STEPS = 64
FFN_TILE_FACTOR = 2
HEAD_ROWS = 256
VMEM_LIMIT_BYTES = 60 * 1024 * 1024

_BF16 = jnp.bfloat16
_F32 = jnp.float32


def _dot(a, b):
    return jnp.dot(a, b, preferred_element_type=_F32)


def _sigmoid_of_twice(half_z):
    return 0.5 * jnp.tanh(half_z) + 0.5


def _layer_norm(y, g, b):
    mu = jnp.mean(y, axis=-1, keepdims=True)
    yc = y - mu
    var = jnp.mean(yc * yc, axis=-1, keepdims=True)
    return yc * jax.lax.rsqrt(var + LN_EPS) * g + b


def _mixer_kernel(nb, tt, n_tiles, batch_major, fresh, n_convert,
                  x_ref, w_in_ref, cw_ref, cb_ref, wg_ref, ba_ref, bx_ref, lam_ref,
                  wb_ref, are_ref, aim_ref, wcr_ref, wci_ref, d_ref, wglu_ref, wout_ref, *rest):
    initial_state, rest = ((), rest) if fresh else (rest[:4], rest[4:])
    to_convert, rest = rest[:n_convert], rest[n_convert:]
    (y1_ref, conv_out_ref, h_ref, sre_ref, sim_ref, *rest) = rest
    converted, rest = rest[:n_convert], rest[n_convert:]
    (xbuf, abuf, hbuf, bure, buim, gpre, *maybe_xt) = rest
    m = nb * tt
    hist = (LRU_CONV - 1) * nb

    @pl.when(pl.program_id(0) == 0)
    def _():
        if fresh:
            xbuf[0:hist, :] = jnp.zeros((hist, D_MODEL), _F32)
            for state in (h_ref, sre_ref, sim_ref):
                state[...] = jnp.zeros(state.shape, _F32)
        else:
            conv0_ref, h0_ref, sre0_ref, sim0_ref = initial_state
            xbuf[0:hist, :] = conv0_ref[...]
            h_ref[...] = h0_ref[...]
            sre_ref[...] = sre0_ref[...]
            sim_ref[...] = sim0_ref[...]

    if batch_major:
        xt, x_sems = maybe_xt
        step = pl.program_id(0)
        slot = step % 2

        def tile_copies(tile, into):
            return [pltpu.make_async_copy(x_ref.at[b, pl.ds(tile * tt, tt), :],
                                          xt.at[into, :, b, :],
                                          x_sems.at[into, b]) for b in range(nb)]

        @pl.when(step == 0)
        def _():
            for copy in tile_copies(0, 0):
                copy.start()

        @pl.when(step + 1 < n_tiles)
        def _():
            for copy in tile_copies(step + 1, 1 - slot):
                copy.start()

        for copy in tile_copies(step, slot):
            copy.wait()
        load_x = lambda: xt[slot].reshape(m, D_MODEL)
    else:
        load_x = lambda: jnp.concatenate([x_ref[:, t, :] for t in range(tt)], axis=0)
    xb = load_x().astype(_BF16)

    lam = lam_ref[...]
    c_lam = (0.5 * LRU_C * LOG2_E) * (jnp.minimum(lam, 0.0) - jnp.log1p(jnp.exp(-jnp.abs(lam))))
    gate_cols = D_MODEL + D_S5

    def lru_input_projection(k):
        blk = slice(k * MXU_TILE, (k + 1) * MXU_TILE)
        for r0 in range(0, m, HEAD_ROWS if k == 0 else m):
            rows = slice(r0, r0 + (HEAD_ROWS if k == 0 else m))
            xbuf[hist + rows.start:hist + rows.stop, blk] = _dot(xb[rows, :], w_in_ref[:, blk])

    lru_input_projection(0)
    for k in range(GATE_BLOCKS):
        blk = slice(k * MXU_TILE, (k + 1) * MXU_TILE)
        if k + 1 < GATE_BLOCKS:
            lru_input_projection(k + 1)
        if k == 0:
            u = _dot(xb, w_in_ref[:, D_MODEL:D_MODEL + D_S5])
            ub = u.astype(_BF16)
            for j in range(S5_BLOCKS):
                bu = _dot(ub[:, j * S5_IN_BLOCK:(j + 1) * S5_IN_BLOCK], wb_ref[j])
                cols = slice(j * S5_STATE_BLOCK, (j + 1) * S5_STATE_BLOCK)
                bure[:, cols] = bu[:, 0:S5_STATE_BLOCK]
                buim[:, cols] = bu[:, S5_STATE_BLOCK:]
        xc = cb_ref[:, blk] + xbuf[0:m, blk] * cw_ref[0:1, blk]
        for j in range(1, LRU_CONV):
            xc = xc + xbuf[j * nb:j * nb + m, blk] * cw_ref[j:j + 1, blk]
        new_hist = xbuf[m:m + hist, blk]
        xbuf[0:hist, blk] = new_hist
        conv_out_ref[:, blk] = new_hist
        half_g = _dot(xc.astype(_BF16), wg_ref[k])
        tanh_r = jnp.tanh(half_g[:, 0:MXU_TILE] + ba_ref[:, blk])
        ig = _sigmoid_of_twice(half_g[:, MXU_TILE:] + bx_ref[:, blk])
        a = jnp.exp2(tanh_r * c_lam[:, blk] + c_lam[:, blk])
        abuf[:, blk] = a
        one_minus_a2 = jnp.maximum((1.0 - a) * (1.0 + a), 0.0)
        mult = one_minus_a2 * jax.lax.rsqrt(jnp.maximum(one_minus_a2, RSQRT_FLOOR))
        hbuf[:, blk] = mult * (ig * xc)
        for half in range(2):
            gblk = slice((2 * k + half) * MXU_TILE, (2 * k + half + 1) * MXU_TILE)
            gpre[:, gblk] = _dot(xb, w_in_ref[:, gate_cols + gblk.start:gate_cols + gblk.stop])

    h = h_ref[...]
    for t in range(tt):
        rows = slice(t * nb, (t + 1) * nb)
        h = abuf[rows, :] * h + hbuf[rows, :]
        hbuf[rows, :] = h
    h_ref[...] = h

    ys = []
    for k in range(S5_BLOCKS):
        cols = slice(k * S5_STATE_BLOCK, (k + 1) * S5_STATE_BLOCK)
        ar = jnp.broadcast_to(are_ref[:, cols], (nb, S5_STATE_BLOCK))
        ai = jnp.broadcast_to(aim_ref[:, cols], (nb, S5_STATE_BLOCK))
        hr = sre_ref[:, cols]
        hi = sim_ref[:, cols]
        for t in range(tt):
            rows = slice(t * nb, (t + 1) * nb)
            hr_new = ar * hr - ai * hi + bure[rows, cols]
            hi_new = ar * hi + ai * hr + buim[rows, cols]
            hr, hi = hr_new, hi_new
            bure[rows, cols] = hr
            buim[rows, cols] = hi
        sre_ref[:, cols] = hr
        sim_ref[:, cols] = hi
        ys.append(_dot(bure[:, cols].astype(_BF16), wcr_ref[k])
                  - _dot(buim[:, cols].astype(_BF16), wci_ref[k]))
    y = jnp.concatenate(ys, axis=1) + d_ref[...] * u
    z = jax.nn.gelu(y).astype(_BF16)
    glu = _dot(z, wglu_ref[...])
    s5_out = glu[:, 0:D_MODEL] * _sigmoid_of_twice(glu[:, D_MODEL:])

    g_lru = _sigmoid_of_twice(gpre[:, 0:D_MODEL])
    g_s5 = _sigmoid_of_twice(gpre[:, D_MODEL:])
    merged = (g_lru * hbuf[...] + g_s5 * s5_out).astype(_BF16)
    mix = _dot(merged, wout_ref[...])
    y1_ref[...] = ALPHA * load_x() + mix

    for src_ref, dst_ref in zip(to_convert, converted):
        dst_ref[...] = src_ref[...].astype(_BF16)


def _ffn_kernel(nb, tt, n_tiles, batch_major, fresh,
                y1_ref, wup_ref, wdown_ref, g1_ref, b1_ref, cw_ref, cb_ref, g2_ref, b2_ref, *rest):
    initial_state, rest = ((), rest) if fresh else (rest[:1], rest[1:])
    (out_ref, conv_out_ref, abuf, *maybe_yt) = rest
    m = nb * tt
    hist = (FFN_CONV - 1) * nb
    step = pl.program_id(0)

    @pl.when(step == 0)
    def _():
        abuf[0:hist, :] = jnp.zeros((hist, D_FF), _F32) if fresh else initial_state[0][...]

    if batch_major:
        yt, y_sems = maybe_yt
        slot = step % 2

        def tile_copies(tile, staged):
            return [pltpu.make_async_copy(yt.at[staged, :, b, :],
                                          out_ref.at[b, pl.ds(tile * tt, tt), :],
                                          y_sems.at[staged, b]) for b in range(nb)]

        @pl.when(step >= 2)
        def _():
            for copy in tile_copies(step - 2, slot):
                copy.wait()

    x = _layer_norm(y1_ref[...], g1_ref[...], b1_ref[...])
    xb = x.astype(_BF16)
    for r0 in range(0, m, HEAD_ROWS):
        abuf[hist + r0:hist + r0 + HEAD_ROWS, 0:D_MODEL] = _dot(xb[r0:r0 + HEAD_ROWS, :], wup_ref[:, 0:D_MODEL])
    abuf[hist:hist + m, D_MODEL:] = _dot(xb, wup_ref[:, D_MODEL:D_FF])
    f = None
    for c in range(D_FF // D_MODEL):
        cols = slice(c * D_MODEL, (c + 1) * D_MODEL)
        ac = cb_ref[:, cols] + abuf[0:m, cols] * cw_ref[0:1, cols]
        for k in range(1, FFN_CONV):
            ac = ac + abuf[k * nb:k * nb + m, cols] * cw_ref[k:k + 1, cols]
        gate = _dot(xb, wup_ref[:, D_FF + c * D_MODEL:D_FF + (c + 1) * D_MODEL])
        hmid = (jax.nn.gelu(ac) * gate).astype(_BF16)
        part = _dot(hmid, wdown_ref[cols, :])
        f = part if f is None else f + part
    new_hist = abuf[m:m + hist, :]
    abuf[0:hist, :] = new_hist
    conv_out_ref[...] = new_hist
    y = _layer_norm(ALPHA * x + f, g2_ref[...], b2_ref[...])
    if batch_major:
        yt[slot] = y.reshape(tt, nb, D_MODEL)
        for copy in tile_copies(step, slot):
            copy.start()

        @pl.when(step == n_tiles - 1)
        def _():
            if n_tiles > 1:
                for copy in tile_copies(step - 1, 1 - slot):
                    copy.wait()
            for copy in tile_copies(step, slot):
                copy.wait()
    else:
        for t in range(tt):
            out_ref[:, t, :] = y[t * nb:(t + 1) * nb, :]


def _resident(shape):
    zeros = (0,) * len(shape)
    return pl.BlockSpec(shape, lambda i: zeros, pipeline_mode=pl.Buffered(1))


def _run_mixer(x, nb, tt, weights, states, batch_major, to_convert=()):
    m = nb * tt
    hist = (LRU_CONV - 1) * nb
    n_rows = nb * x.shape[1]
    n_tiles = n_rows // m
    tile = pl.BlockSpec((m, D_MODEL), lambda i: (i, 0))
    if batch_major:
        x_spec = pl.BlockSpec(memory_space=pl.ANY)
    else:
        x_spec = pl.BlockSpec((nb, tt, D_MODEL), lambda i: (0, i, 0))
    resident = tuple(weights) + tuple(states or ())
    slabs = []
    for w in to_convert:
        assert w.shape[0] % (n_tiles * BF16_SUBLANES) == 0, (w.shape, n_tiles)
        slabs.append(pl.BlockSpec((w.shape[0] // n_tiles, w.shape[1]), lambda i: (i, 0)))
    operands = (x,) + resident + tuple(to_convert)
    in_specs = [x_spec] + [_resident(a.shape) for a in resident] + slabs
    state_shapes = (
        jax.ShapeDtypeStruct((hist, D_MODEL), _F32),
        jax.ShapeDtypeStruct((nb, D_MODEL), _F32),
        jax.ShapeDtypeStruct((nb, D_STATE), _F32),
        jax.ShapeDtypeStruct((nb, D_STATE), _F32),
    )
    out_shape = ((jax.ShapeDtypeStruct((n_rows, D_MODEL), _F32),) + state_shapes
                 + tuple(jax.ShapeDtypeStruct(w.shape, _BF16) for w in to_convert))
    out_specs = [tile] + [pl.BlockSpec(s.shape, lambda i: (0, 0)) for s in state_shapes] + slabs
    scratch_shapes = [
        pltpu.VMEM((hist + m, D_MODEL), _F32),
        pltpu.VMEM((m, D_MODEL), _F32),
        pltpu.VMEM((m, D_MODEL), _F32),
        pltpu.VMEM((m, D_STATE), _F32),
        pltpu.VMEM((m, D_STATE), _F32),
        pltpu.VMEM((m, 2 * D_MODEL), _F32),
    ]
    if batch_major:
        scratch_shapes += [pltpu.VMEM((2, tt, nb, D_MODEL), _F32), pltpu.SemaphoreType.DMA((2, nb))]
    return pl.pallas_call(
        functools.partial(_mixer_kernel, nb, tt, n_tiles, batch_major, states is None, len(to_convert)),
        out_shape=out_shape,
        grid=(n_tiles,),
        in_specs=in_specs,
        out_specs=out_specs,
        scratch_shapes=scratch_shapes,
        compiler_params=pltpu.CompilerParams(
            dimension_semantics=("arbitrary",), vmem_limit_bytes=VMEM_LIMIT_BYTES),
        name=f"mixer_nb{nb}",
    )(*operands)


def _run_ffn(x_rows, nb, tt, weights, conv0, batch_major):
    n_rows = x_rows.shape[0]
    m = nb * tt
    hist = (FFN_CONV - 1) * nb
    tile = pl.BlockSpec((m, D_MODEL), lambda i: (i, 0))
    operands = (x_rows,) + tuple(weights) + (() if conv0 is None else (conv0,))
    in_specs = [tile] + [_resident(a.shape) for a in operands[1:]]
    y_shape = jax.ShapeDtypeStruct((nb, n_rows // nb, D_MODEL), _F32)
    if batch_major:
        y_spec = pl.BlockSpec(memory_space=pl.ANY)
    else:
        y_spec = pl.BlockSpec((nb, tt, D_MODEL), lambda i: (0, i, 0))
    out_shape = (y_shape, jax.ShapeDtypeStruct((hist, D_FF), _F32))
    out_specs = [y_spec, pl.BlockSpec((hist, D_FF), lambda i: (0, 0))]
    scratch_shapes = [pltpu.VMEM((hist + m, D_FF), _F32)]
    if batch_major:
        scratch_shapes += [pltpu.VMEM((2, tt, nb, D_MODEL), _F32), pltpu.SemaphoreType.DMA((2, nb))]
    return pl.pallas_call(
        functools.partial(_ffn_kernel, nb, tt, n_rows // m, batch_major, conv0 is None),
        out_shape=out_shape,
        grid=(n_rows // m,),
        in_specs=in_specs,
        out_specs=out_specs,
        scratch_shapes=scratch_shapes,
        compiler_params=pltpu.CompilerParams(
            dimension_semantics=("arbitrary",), vmem_limit_bytes=VMEM_LIMIT_BYTES),
        name=f"ffn_nb{nb}",
    )(*operands)


def _block_diag(blocks, n_per_tile):
    n, r, c = blocks.shape
    tiles = n // n_per_tile
    eye = jnp.eye(n_per_tile, dtype=blocks.dtype)
    b = blocks.reshape(tiles, n_per_tile, r, c)
    return jnp.einsum("tgrc,gh->tgrhc", b, eye).reshape(tiles, n_per_tile * r, n_per_tile * c)


def _s5_discretise(a_re, a_im, log_dt, b_re, b_im):
    dt = jnp.exp(log_dt)[:, None]
    mag = jnp.exp(a_re * dt)
    ab_re = mag * jnp.cos(a_im * dt)
    ab_im = mag * jnp.sin(a_im * dt)
    nr = ab_re - 1.0
    ni = ab_im
    den = a_re * a_re + a_im * a_im
    coef_re = (nr * a_re + ni * a_im) / den
    coef_im = (ni * a_re - nr * a_im) / den
    bb_re = coef_re[..., None] * b_re - coef_im[..., None] * b_im
    bb_im = coef_re[..., None] * b_im + coef_im[..., None] * b_re
    return ab_re, ab_im, bb_re, bb_im


def _time_major(x):
    b, l, c = x.shape
    return jnp.transpose(x, (1, 0, 2)).reshape(l * b, c)


def _batch_major(rows, b):
    n, c = rows.shape
    return jnp.transpose(rows.reshape(n // b, b, c), (1, 0, 2))


def _layer(x, state, tt, mixer_w, ffn_mats, ffn_vecs):
    nb = x.shape[0]
    batch_major = tt % SUBLANES == 0
    mixer_state = ffn_conv0 = None
    if state is not None:
        lru_conv0, lru_h0, s5_re0, s5_im0, ffn_conv = state
        mixer_state = (_time_major(lru_conv0), lru_h0, s5_re0.reshape(nb, D_STATE), s5_im0.reshape(nb, D_STATE))
        ffn_conv0 = _time_major(ffn_conv)
    to_convert = tuple(ffn_mats) if all(w.dtype == _F32 for w in ffn_mats) else ()
    x1, conv_out, h_out, sre, sim, *converted = _run_mixer(x, nb, tt, mixer_w, mixer_state, batch_major, to_convert)
    ffn_mats = tuple(converted) if to_convert else tuple(ffn_mats)
    ffn_tt = tt * FFN_TILE_FACTOR if batch_major else tt
    y, fconv_out = _run_ffn(x1, nb, ffn_tt, ffn_mats + tuple(ffn_vecs), ffn_conv0, batch_major)
    return (y, _batch_major(conv_out, nb), h_out,
            sre.reshape(nb, S5_GROUPS, S5_STATE), sim.reshape(nb, S5_GROUPS, S5_STATE),
            _batch_major(fconv_out, nb), ffn_mats)


def kernel(x_prompt, x_sample, state_lru_conv, state_lru_h, state_s5_re, state_s5_im, state_ffn_conv, w_in, lru_conv_w, lru_conv_b, lru_wa, lru_ba, lru_wx, lru_bx, lru_lambda, s5_a_re, s5_a_im, s5_log_dt, s5_b_re, s5_b_im, s5_c_re, s5_c_im, s5_d, w_glu, w_out, ln1_g, ln1_b, w_up, ffn_conv_w, ffn_conv_b, w_down, ln2_g, ln2_b):
    xp, xs = x_prompt, x_sample
    outs_p, outs_s = [], []
    for l in range(DEPTH):
        row = lambda v: v[l].reshape(1, -1)
        heads_per_tile = MXU_TILE // LRU_HEAD_DIM
        wg = (0.5 * jnp.concatenate([_block_diag(lru_wa[l], heads_per_tile),
                                     _block_diag(lru_wx[l], heads_per_tile)], axis=2)).astype(_BF16)
        w_in_scale = jnp.concatenate([jnp.ones((D_MODEL + D_S5,), _F32), jnp.full((2 * D_MODEL,), 0.5, _F32)])
        w_glu_scale = jnp.concatenate([jnp.ones((D_MODEL,), _F32), jnp.full((D_MODEL,), 0.5, _F32)])
        ab_re, ab_im, bb_re, bb_im = _s5_discretise(s5_a_re[l], s5_a_im[l], s5_log_dt[l], s5_b_re[l], s5_b_im[l])
        groups_per_block = S5_IN_BLOCK // S5_GROUP
        to_in = lambda bb: _block_diag(jnp.swapaxes(bb, 1, 2), groups_per_block)
        to_out = lambda cc: _block_diag(jnp.swapaxes(cc, 1, 2), groups_per_block)
        wb = jnp.concatenate([to_in(bb_re), to_in(bb_im)], axis=2).astype(_BF16)
        mixer_w = (
            (w_in[l] * w_in_scale).astype(_BF16), lru_conv_w[l], row(lru_conv_b), wg,
            0.5 * row(lru_ba), 0.5 * row(lru_bx), row(lru_lambda),
            wb, ab_re.reshape(1, D_STATE), ab_im.reshape(1, D_STATE),
            to_out(s5_c_re[l]).astype(_BF16), to_out(s5_c_im[l]).astype(_BF16), row(s5_d),
            (w_glu[l] * w_glu_scale).astype(_BF16), w_out[l].astype(_BF16),
        )
        ffn_vecs = (row(ln1_g), row(ln1_b), ffn_conv_w[l], row(ffn_conv_b), row(ln2_g), row(ln2_b))

        xp, *st_p, ffn_mats = _layer(xp, None, PROMPT_TILE_STEPS, mixer_w, (w_up[l], w_down[l]), ffn_vecs)
        outs_p.append(st_p)
        carried = (state_lru_conv[l], state_lru_h[l], state_s5_re[l], state_s5_im[l], state_ffn_conv[l])
        xs, *st_s, _ = _layer(xs, carried, xs.shape[1], mixer_w, ffn_mats, ffn_vecs)
        outs_s.append(st_s)
    stack = lambda outs, j: jnp.stack([o[j] for o in outs])
    return (xp, xs,
            *(stack(outs_p, j) for j in range(5)),
            *(stack(outs_s, j) for j in range(5)))
```
